```python
import math
import jax, jax.numpy as jnp
from jax import lax
import numpy as np

D_MODEL = 1024
BATCH = 8
SEQ = 4096
DEPTH = 2
DEC_BATCH = 32
DEC_SEQ = 1
PAST_LEN = 16384
PAGE_SIZE = 128

SSM_D_INNER = D_MODEL
SSM_HEADDIM = 64
SSM_HEADS = SSM_D_INNER // SSM_HEADDIM
SSM_GROUPS = 2
SSM_D_STATE = 128
SSM_CONV = 4
SSM_CONV_DIM = SSM_D_INNER + 2 * SSM_GROUPS * SSM_D_STATE
SSM_CHUNK = 128
ATT_HEADS = 8
ATT_KV_HEADS = 4
ATT_GROUP = ATT_HEADS // ATT_KV_HEADS
ATT_QK_DIM = D_MODEL // ATT_HEADS // 2
ATT_V_DIM = 2 * ATT_QK_DIM
ATT_WIDTH = ATT_HEADS * ATT_V_DIM
Q_BLOCK = 128
FFN_HIDDEN = -(-8 * D_MODEL // (3 * 256)) * 256
RMS_EPS = 1e-6
Z_END = SSM_D_INNER
XBC_END = Z_END + SSM_CONV_DIM
DT_END = XBC_END + SSM_HEADS
Q_END = DT_END + ATT_HEADS * 2 * ATT_QK_DIM
K_END = Q_END + ATT_KV_HEADS * 2 * ATT_QK_DIM
V_END = K_END + ATT_KV_HEADS * ATT_V_DIM
IN_COLS = V_END + 2 * D_MODEL

kernel_name = 'hybrid_ssd_diffattn_decoder_step'


def rms_norm(x, g, eps=RMS_EPS):
    xf = x.astype(jnp.float32)
    y = xf * lax.rsqrt(jnp.mean(xf * xf, axis=-1, keepdims=True) + eps)
    return (y * g.astype(jnp.float32)).astype(x.dtype)


def causal_conv(xbc, conv_prev, conv_w, conv_b):
    full = jnp.concatenate([conv_prev.astype(xbc.dtype), xbc], axis=1)
    ch = xbc.shape[-1]
    out = lax.conv_general_dilated(
        full, conv_w.astype(xbc.dtype).reshape(SSM_CONV, 1, ch), window_strides=(1,),
        padding='VALID', dimension_numbers=('NWC', 'WIO', 'NWC'), feature_group_count=ch)
    return jax.nn.silu(out + conv_b.astype(xbc.dtype)), full[:, -(SSM_CONV - 1):]


def ssd_scan(x, dt, a, bm, cm, s0):
    f32 = jnp.float32
    b, L, H, P = x.shape
    G, N = bm.shape[2], bm.shape[3]
    R = H // G
    chunk = min(SSM_CHUNK, L)
    pad = (-L) % chunk
    x, dt, bm, cm = x.astype(f32), dt.astype(f32), bm.astype(f32), cm.astype(f32)
    if pad:
        padw = lambda t: jnp.pad(t, [(0, 0), (0, pad)] + [(0, 0)] * (t.ndim - 2))
        x, dt, bm, cm = padw(x), padw(dt), padw(bm), padw(cm)
    nc = (L + pad) // chunk
    xc = x.reshape(b, nc, chunk, G, R, P)
    dtc = dt.reshape(b, nc, chunk, G, R)
    acum = jnp.cumsum(dtc * a.reshape(G, R), axis=2)
    xdt = xc * dtc[..., None]
    bc = bm.reshape(b, nc, chunk, G, N)
    cc = cm.reshape(b, nc, chunk, G, N)
    causal = jnp.tril(jnp.ones((chunk, chunk), bool))
    seg = acum[:, :, :, None] - acum[:, :, None, :]
    decay = jnp.exp(jnp.where(causal[:, :, None, None], seg, -jnp.inf))
    cb = jnp.einsum('bclgn,bcsgn->bclsg', cc, bc)
    y_diag = jnp.einsum('bclsgr,bcsgrp->bclgrp', cb[..., None] * decay, xdt)
    to_end = jnp.exp(acum[:, :, -1:] - acum)
    st = jnp.einsum('bclgn,bclgrp->bcgrpn', bc, xdt * to_end[..., None])
    chunk_decay = jnp.exp(acum[:, :, -1])

    def step(s, inp):
        st_c, dec_c = inp
        return s * dec_c[..., None, None] + st_c, s

    s_final, s_in = lax.scan(step, s0.astype(f32).reshape(b, G, R, P, N),
                             (jnp.moveaxis(st, 1, 0), jnp.moveaxis(chunk_decay, 1, 0)))
    s_in = jnp.moveaxis(s_in, 0, 1)
    y_off = jnp.einsum('bclgn,bcgrpn->bclgrp', cc, s_in) * jnp.exp(acum)[..., None]
    y = (y_diag + y_off).reshape(b, nc * chunk, H, P)[:, :L]
    return y, s_final.reshape(b, H, P, N)


def gated_group_rmsnorm(y, z, g, out_dtype):
    yz = y * jax.nn.silu(z.astype(jnp.float32))
    shp = yz.shape
    yz = yz.reshape(shp[:-1] + (SSM_GROUPS, shp[-1] // SSM_GROUPS))
    yz = yz * lax.rsqrt(jnp.mean(yz * yz, axis=-1, keepdims=True) + RMS_EPS)
    return (yz.reshape(shp) * g.astype(jnp.float32)).astype(out_dtype)


def alibi_slopes():
    h = jnp.arange(1, ATT_HEADS + 1, dtype=jnp.float32)
    return (2.0 ** (-8.0 * h / ATT_HEADS)).reshape(ATT_KV_HEADS, ATT_GROUP)


def diff_attn(q, k, v, q_pos, k_pos, lam):
    s = jnp.einsum('bqgrcd,bkgcd->bgrcqk', q, k).astype(jnp.float32) * (ATT_QK_DIM ** -0.5)
    dist = (q_pos[:, None] - k_pos[None, :]).astype(jnp.float32)
    bias = -alibi_slopes()[:, :, None, None, None] * dist
    s = jnp.where(dist >= 0, s + bias, -jnp.inf)
    p = jax.nn.softmax(s, axis=-1)
    w = p[:, :, :, 0] - lam * p[:, :, :, 1]
    return jnp.einsum('bgrqk,bkgv->bqgrv', w.astype(v.dtype), v)


def diff_attn_prompt(q, k, v, lam):
    b, L = q.shape[:2]
    qb = min(Q_BLOCK, L)
    nb = L // qb
    qs = jnp.moveaxis(q.reshape((b, nb, qb) + q.shape[2:]), 1, 0)
    qpos = jnp.arange(L).reshape(nb, qb)
    kpos = jnp.arange(L)
    o = lax.map(lambda a: diff_attn(a[0], k, v, a[1], kpos, lam), (qs, qpos))
    return jnp.moveaxis(o, 0, 1).reshape((b, L) + o.shape[3:])


def mixer(h, layer, conv_prev, ssm_prev, past_k, past_v, w_in, conv_w, conv_b, dt_bias, a_log,
          d_skip, ssm_norm_g, lam_vecs, subln_g, w_branch, w_out):
    f32 = jnp.float32
    b, L, _ = h.shape
    proj = h @ w_in
    z, xbc, dt_raw, q, k, v, gates = jnp.split(
        proj, [Z_END, XBC_END, DT_END, Q_END, K_END, V_END], axis=-1)
    xbc, conv_new = causal_conv(xbc, conv_prev, conv_w, conv_b)
    xs, bm, cm = jnp.split(xbc, [SSM_D_INNER, SSM_D_INNER + SSM_GROUPS * SSM_D_STATE], axis=-1)
    xs = xs.reshape(b, L, SSM_HEADS, SSM_HEADDIM)
    bm = bm.reshape(b, L, SSM_GROUPS, SSM_D_STATE)
    cm = cm.reshape(b, L, SSM_GROUPS, SSM_D_STATE)
    dt = jax.nn.softplus(dt_raw.astype(f32) + dt_bias.astype(f32))
    a = -jnp.exp(a_log.astype(f32))
    y, ssm_new = ssd_scan(xs, dt, a, bm, cm, ssm_prev)
    y = y + d_skip.astype(f32)[:, None] * xs.astype(f32)
    y_ssm = gated_group_rmsnorm(y.reshape(b, L, SSM_D_INNER), z, ssm_norm_g, h.dtype)
    q = q.reshape(b, L, ATT_KV_HEADS, ATT_GROUP, 2, ATT_QK_DIM)
    k = k.reshape(b, L, ATT_KV_HEADS, 2, ATT_QK_DIM)
    v = v.reshape(b, L, ATT_KV_HEADS, ATT_V_DIM)
    lam_init = 0.8 - 0.6 * math.exp(-0.3 * layer)
    lv = lam_vecs.astype(f32)
    lam = jnp.exp(jnp.sum(lv[0] * lv[1])) - jnp.exp(jnp.sum(lv[2] * lv[3])) + lam_init
    if past_k is None:
        o = diff_attn_prompt(q, k, v, lam)
    else:
        p0 = past_k.shape[1]
        kk = jnp.concatenate([past_k.astype(k.dtype), k], axis=1)
        vv = jnp.concatenate([past_v.astype(v.dtype), v], axis=1)
        o = diff_attn(q, kk, vv, p0 + jnp.arange(L), jnp.arange(p0 + L), lam)
    o = rms_norm(o, subln_g) * (1.0 - lam_init)
    y_att = o.reshape(b, L, ATT_WIDTH)
    br = jnp.stack([y_ssm, y_att], axis=2)
    bp = jnp.einsum('blnk,nkd->blnd', br, w_branch)
    gate = jax.nn.sigmoid(gates.reshape(b, L, 2, D_MODEL))
    merged = jnp.sum(gate * bp, axis=2)
    return merged @ w_out, conv_new, ssm_new, k, v


def trunk_layer(x, c, layer, conv_prev, ssm_prev, past_k, past_v, w_ada, b_ada, g_pre_mix,
                g_post_mix, g_pre_ffn, g_post_ffn, w_in, conv_w, conv_b, dt_bias, a_log, d_skip,
                ssm_norm_g, lam_vecs, subln_g, w_branch, w_out, w_gu, w_down):
    b = x.shape[0]
    mod = (jax.nn.silu(c) @ w_ada + b_ada).reshape(b, 6, 1, D_MODEL)
    shift1, scale1, gate1, shift2, scale2, gate2 = (mod[:, i] for i in range(6))
    h = rms_norm(x, g_pre_mix) * (1 + scale1) + shift1
    m, conv_new, ssm_new, k, v = mixer(h, layer, conv_prev, ssm_prev, past_k, past_v, w_in, conv_w,
                                       conv_b, dt_bias, a_log, d_skip, ssm_norm_g, lam_vecs,
                                       subln_g, w_branch, w_out)
    x = x + gate1 * rms_norm(m, g_post_mix)
    h = rms_norm(x, g_pre_ffn) * (1 + scale2) + shift2
    gg, uu = jnp.split(h @ w_gu, 2, axis=-1)
    f = (jax.nn.silu(gg) * uu) @ w_down
    x = x + gate2 * rms_norm(f, g_post_ffn)
    return x, conv_new, ssm_new, k, v


def setup_inputs(seed: int = 0) -> dict:
    key = jax.random.key(seed)
    ks = jax.random.split(key, 32)
    f32 = jnp.float32
    nrm = lambda k, shape, s=1.0: s * jax.random.normal(k, shape, f32)
    n_pages = PAST_LEN // PAGE_SIZE
    n_used = DEC_BATCH * n_pages
    n_pool = n_used + max(1, n_used // 4)
    page_table = jax.random.permutation(ks[8], n_pool)[:n_used].reshape(DEC_BATCH, n_pages).astype(jnp.int32)
    dt0 = jnp.exp(jax.random.uniform(ks[17], (DEPTH, SSM_HEADS), f32, math.log(1e-3), math.log(1e-1)))
    dt_bias = dt0 + jnp.log(-jnp.expm1(-dt0))
    a_log = jnp.log(jax.random.uniform(ks[18], (DEPTH, SSM_HEADS), f32, 1.0, 16.0))
    return {
        'x_prompt': nrm(ks[0], (BATCH, SEQ, D_MODEL)),
        'x_sample': nrm(ks[1], (DEC_BATCH, DEC_SEQ, D_MODEL)),
        'cache_k': nrm(ks[2], (DEPTH, n_pool, PAGE_SIZE, ATT_KV_HEADS, 2, ATT_QK_DIM)),
        'cache_v': nrm(ks[3], (DEPTH, n_pool, PAGE_SIZE, ATT_KV_HEADS, ATT_V_DIM)),
        'state_conv': nrm(ks[4], (DEPTH, DEC_BATCH, SSM_CONV - 1, SSM_CONV_DIM)),
        'state_ssm': nrm(ks[5], (DEPTH, DEC_BATCH, SSM_HEADS, SSM_HEADDIM, SSM_D_STATE), 0.5),
        'page_table': page_table,
        'c_prompt': nrm(ks[6], (BATCH, D_MODEL)),
        'c_sample': nrm(ks[7], (DEC_BATCH, D_MODEL)),
        'w_ada': nrm(ks[9], (DEPTH, D_MODEL, 6 * D_MODEL), D_MODEL ** -0.5),
        'b_ada': nrm(ks[10], (DEPTH, 6 * D_MODEL), 0.01),
        'g_pre_mix': 1.0 + nrm(ks[11], (DEPTH, D_MODEL), 0.05),
        'g_post_mix': 1.0 + nrm(ks[12], (DEPTH, D_MODEL), 0.05),
        'g_pre_ffn': 1.0 + nrm(ks[13], (DEPTH, D_MODEL), 0.05),
        'g_post_ffn': 1.0 + nrm(ks[14], (DEPTH, D_MODEL), 0.05),
        'w_in': nrm(ks[15], (DEPTH, D_MODEL, IN_COLS), D_MODEL ** -0.5),
        'conv_w': nrm(ks[16], (DEPTH, SSM_CONV, SSM_CONV_DIM), SSM_CONV ** -0.5),
        'conv_b': nrm(ks[19], (DEPTH, SSM_CONV_DIM), 0.01),
        'dt_bias': dt_bias,
        'a_log': a_log,
        'd_skip': 1.0 + nrm(ks[20], (DEPTH, SSM_HEADS), 0.1),
        'ssm_norm_g': 1.0 + nrm(ks[21], (DEPTH, SSM_D_INNER), 0.05),
        'lam_vecs': nrm(ks[22], (DEPTH, 4, ATT_QK_DIM), 0.1),
        'subln_g': 1.0 + nrm(ks[23], (DEPTH, ATT_V_DIM), 0.05),
        'w_branch': nrm(ks[24], (DEPTH, 2, ATT_WIDTH, D_MODEL), ATT_WIDTH ** -0.5),
        'w_out': nrm(ks[25], (DEPTH, D_MODEL, D_MODEL), D_MODEL ** -0.5),
        'w_gu': nrm(ks[26], (DEPTH, D_MODEL, 2 * FFN_HIDDEN), D_MODEL ** -0.5),
        'w_down': nrm(ks[27], (DEPTH, FFN_HIDDEN, D_MODEL), FFN_HIDDEN ** -0.5),
    }


def reference(x_prompt, x_sample, cache_k, cache_v, state_conv, state_ssm, page_table, c_prompt,
              c_sample, w_ada, b_ada, g_pre_mix, g_post_mix, g_pre_ffn, g_post_ffn, w_in, conv_w,
              conv_b, dt_bias, a_log, d_skip, ssm_norm_g, lam_vecs, subln_g, w_branch, w_out, w_gu,
              w_down):
    n_seq, n_pages = page_table.shape
    past = n_pages * PAGE_SIZE
    bp = x_prompt.shape[0]
    conv0 = jnp.zeros((bp, SSM_CONV - 1, SSM_CONV_DIM), x_prompt.dtype)
    ssm0 = jnp.zeros((bp, SSM_HEADS, SSM_HEADDIM, SSM_D_STATE), jnp.float32)
    xp, xs = x_prompt, x_sample
    kp_l, vp_l, cp_l, sp_l, ks_l, vs_l, cs_l, ss_l = [], [], [], [], [], [], [], []
    for l in range(DEPTH):
        lw = (w_ada[l], b_ada[l], g_pre_mix[l], g_post_mix[l], g_pre_ffn[l], g_post_ffn[l], w_in[l],
              conv_w[l], conv_b[l], dt_bias[l], a_log[l], d_skip[l], ssm_norm_g[l], lam_vecs[l],
              subln_g[l], w_branch[l], w_out[l], w_gu[l], w_down[l])
        xp, cp, sp, kp, vp = trunk_layer(xp, c_prompt, l, conv0, ssm0, None, None, *lw)
        past_k = cache_k[l, page_table].reshape(n_seq, past, ATT_KV_HEADS, 2, ATT_QK_DIM)
        past_v = cache_v[l, page_table].reshape(n_seq, past, ATT_KV_HEADS, ATT_V_DIM)
        xs, cs, ss, ksn, vsn = trunk_layer(xs, c_sample, l, state_conv[l], state_ssm[l], past_k, past_v, *lw)
        kp_l.append(kp); vp_l.append(vp); cp_l.append(cp); sp_l.append(sp)
        ks_l.append(ksn); vs_l.append(vsn); cs_l.append(cs); ss_l.append(ss)
    k_prompt, v_prompt = jnp.stack(kp_l), jnp.stack(vp_l)
    conv_prompt, ssm_prompt = jnp.stack(cp_l), jnp.stack(sp_l)
    k_sample, v_sample = jnp.stack(ks_l), jnp.stack(vs_l)
    conv_sample, ssm_sample = jnp.stack(cs_l), jnp.stack(ss_l)
    return (xp, xs, k_prompt, v_prompt, conv_prompt, ssm_prompt, k_sample, v_sample, conv_sample, ssm_sample)
```

```python
import functools
import math

import numpy as np
import jax
import jax.numpy as jnp
from jax import lax
from jax.experimental import pallas as pl
from jax.experimental.pallas import tpu as pltpu

D_MODEL = 1024
SSM_HEADDIM = 64
SSM_HEADS = 16
SSM_GROUPS = 2
SSM_D_STATE = 128
SSM_CONV = 4
SSM_D_INNER = SSM_HEADS * SSM_HEADDIM
SSM_BC = SSM_GROUPS * SSM_D_STATE
SSM_CONV_DIM = SSM_D_INNER + 2 * SSM_BC
SSM_GROUP_WIDTH = SSM_D_INNER // SSM_GROUPS
ATT_HEADS = 8
ATT_KV_HEADS = 4
ATT_GROUP = ATT_HEADS // ATT_KV_HEADS
ATT_QK_DIM = 64
ATT_V_DIM = 128
ATT_WIDTH = ATT_HEADS * ATT_V_DIM
ATT_Q_COLS = ATT_HEADS * 2 * ATT_QK_DIM
ATT_K_COLS = ATT_KV_HEADS * 2 * ATT_QK_DIM
ATT_V_COLS = ATT_KV_HEADS * ATT_V_DIM
ATT_SCORE_COLS = ATT_KV_HEADS * ATT_GROUP * 2
FFN_HIDDEN = 2816
RMS_EPS = 1e-6
PAGE_SIZE = 128
LANES = 128
SUBLANES = 8
DT_PAD = LANES

_SEG_Z = (0, SSM_D_INNER)
_SEG_XBC = (_SEG_Z[1], _SEG_Z[1] + SSM_CONV_DIM)
_SEG_Q = (_SEG_XBC[1], _SEG_XBC[1] + ATT_Q_COLS)
_SEG_K = (_SEG_Q[1], _SEG_Q[1] + ATT_K_COLS)
_SEG_V = (_SEG_K[1], _SEG_K[1] + ATT_V_COLS)
_SEG_G = (_SEG_V[1], _SEG_V[1] + 2 * D_MODEL)
_SEG_DT = (_SEG_G[1], _SEG_G[1] + DT_PAD)
IN_COLS_PAD = _SEG_DT[1]

VMEM_LIMIT = 56 * 1024 * 1024

f32 = jnp.float32
bf16 = jnp.bfloat16


def _cparams(sem):
    return pltpu.CompilerParams(dimension_semantics=sem, vmem_limit_bytes=VMEM_LIMIT)


def _rms(x):
    return x * lax.rsqrt(jnp.mean(x * x, axis=-1, keepdims=True) + RMS_EPS)


def _silu(x):
    return x * jax.nn.sigmoid(x)


def _softplus(x):
    return jnp.maximum(x, 0.0) + jnp.log(1.0 + jnp.exp(-jnp.abs(x)))


def _split_bf16(x, terms):
    parts = []
    rem = x
    for _ in range(terms):
        p = rem.astype(bf16)
        parts.append(p)
        rem = rem - p.astype(f32)
    return parts


def _dot(a, b):
    return jnp.dot(a, b, preferred_element_type=f32)


def _dot_nt(a, b):
    return lax.dot_general(a, b, (((1,), (1,)), ((), ())), preferred_element_type=f32)


def _dot_tn(a, b):
    return lax.dot_general(a, b, (((0,), (0,)), ((), ())), preferred_element_type=f32)


def _dot_split(x, w, terms):
    acc = None
    for p in _split_bf16(x, terms):
        t = _dot(p, w)
        acc = t if acc is None else acc + t
    return acc


def _ada_kernel(c_ref, w_ref, b_ref, o_ref):
    s = _silu(c_ref[...])
    o_ref[...] = _dot(s.astype(bf16), w_ref[...]) + b_ref[...]


def _ada_mod(c_all, w_ada_b, b_ada):
    depth = w_ada_b.shape[0]
    rows = c_all.shape[0]
    return pl.pallas_call(
        _ada_kernel,
        grid=(depth, 6),
        in_specs=[
            pl.BlockSpec((rows, D_MODEL), lambda l, j: (0, 0)),
            pl.BlockSpec((None, D_MODEL, D_MODEL), lambda l, j: (l, 0, j)),
            pl.BlockSpec((None, 1, D_MODEL), lambda l, j: (l, 0, j)),
        ],
        out_specs=pl.BlockSpec((None, None, rows, D_MODEL), lambda l, j: (l, j, 0, 0)),
        out_shape=jax.ShapeDtypeStruct((depth, 6, rows, D_MODEL), f32),
        compiler_params=_cparams(("arbitrary", "arbitrary")),
        name="ada_mod",
    )(c_all, w_ada_b, b_ada.reshape(depth, 1, 6 * D_MODEL))


def _inproj_kernel(x_ref, mod_ref, g_ref, w_ref, z_ref, xbc_ref, q_ref, k_ref, v_ref, kb_ref,
                   vb_ref, gt_ref, dt_ref):
    h = _rms(x_ref[...]) * g_ref[...]
    h = h * (1.0 + mod_ref[1]) + mod_ref[0]
    hb = h.astype(bf16)

    def seg(lo_hi):
        return _dot(hb, w_ref[:, lo_hi[0]:lo_hi[1]])

    z_ref[...] = seg(_SEG_Z)
    xbc_ref[...] = seg(_SEG_XBC)
    q_ref[...] = seg(_SEG_Q).astype(bf16)
    kk = seg(_SEG_K)
    k_ref[...] = kk
    kb_ref[...] = kk.astype(bf16)
    vv = seg(_SEG_V)
    v_ref[...] = vv
    vb_ref[...] = vv.astype(bf16)
    gt_ref[...] = seg(_SEG_G)
    dt_ref[...] = seg(_SEG_DT)


def _in_proj(x, mod, g_pre, w_cat, layer, tl):
    nb, L, _ = x.shape
    r = mod.shape[2]
    widths = [(SSM_D_INNER, f32), (SSM_CONV_DIM, f32), (ATT_Q_COLS, bf16), (ATT_K_COLS, f32),
              (ATT_V_COLS, f32), (ATT_K_COLS, bf16), (ATT_V_COLS, bf16), (2 * D_MODEL, f32),
              (DT_PAD, f32)]
    tok = lambda w: pl.BlockSpec((None, tl, w), lambda b, i: (b, i, 0))
    return pl.pallas_call(
        _inproj_kernel,
        grid=(nb, L // tl),
        in_specs=[
            tok(D_MODEL),
            pl.BlockSpec((6, None, r, D_MODEL), lambda b, i: (0, b, 0, 0)),
            pl.BlockSpec((None, 1, D_MODEL), lambda b, i: (layer, 0, 0)),
            pl.BlockSpec((None, D_MODEL, IN_COLS_PAD), lambda b, i: (layer, 0, 0),
                         pipeline_mode=pl.Buffered(1)),
        ],
        out_specs=[tok(w) for w, _ in widths],
        out_shape=[jax.ShapeDtypeStruct((nb, L, w), dt) for w, dt in widths],
        compiler_params=_cparams(("parallel", "parallel")),
        name="in_proj",
    )(x, mod, g_pre, w_cat)


def _ssd_kernel(xbc_ref, z_ref, dt_ref, cw_ref, cb_ref, dtb_ref, alog_ref, dsk_ref, ng_ref, e_ref,
                et_ref, y_ref, st_ref, xpad_ref, *, T):
    c = pl.program_id(1)

    @pl.when(c == 0)
    def _():
        xpad_ref[0:SUBLANES, :] = jnp.zeros((SUBLANES, SSM_CONV_DIM), f32)
        st_ref[...] = jnp.zeros_like(st_ref)

    xr = xbc_ref[...]
    xpad_ref[SUBLANES:SUBLANES + T, :] = xr
    acc = xr * cw_ref[SSM_CONV - 1:SSM_CONV, :]
    for lag in range(1, SSM_CONV):
        acc = acc + (xpad_ref[SUBLANES - lag:SUBLANES - lag + T, :]
                     * cw_ref[SSM_CONV - 1 - lag:SSM_CONV - lag, :])
    xpad_ref[0:SUBLANES, :] = xpad_ref[T:T + SUBLANES, :]
    conv = _silu(acc + cb_ref[...])
    xs = conv[:, :SSM_D_INNER]
    bm = conv[:, SSM_D_INNER:SSM_D_INNER + SSM_BC].astype(bf16)
    cm = conv[:, SSM_D_INNER + SSM_BC:].astype(bf16)

    dt = _softplus(dt_ref[...] + dtb_ref[...])
    da = dt * (-jnp.exp(alog_ref[...]))
    row = lax.broadcasted_iota(jnp.int32, (T, T), 0)
    col = lax.broadcasted_iota(jnp.int32, (T, T), 1)
    causal = row >= col
    tri = causal.astype(bf16)
    acum = None
    for p in _split_bf16(da, 3):
        t = _dot(tri, p)
        acum = t if acum is None else acum + t
    acum_t = acum.T
    last = acum[T - 1:T, :]
    ea = jnp.exp(acum)
    to_end = jnp.exp(last - acum)

    e = e_ref[...]
    dt_f = _dot_split(dt, e, 2)
    ea_f = _dot_split(ea, e, 2)
    te_f = _dot_split(to_end, e, 2)
    xdt = xs * dt_f
    xdt_b = xdt.astype(bf16)
    xdtw_b = (xdt * te_f).astype(bf16)

    cd = jnp.exp(jnp.broadcast_to(acum_t[:, T - 1:T], (LANES, SSM_D_STATE)))
    cd_rows = None
    et = et_ref[...]
    for p in _split_bf16(cd, 2):
        t = _dot(et, p)
        cd_rows = t if cd_rows is None else cd_rows + t

    hpg = SSM_HEADS // SSM_GROUPS
    y_parts = []
    for g in range(SSM_GROUPS):
        bg = bm[:, g * SSM_D_STATE:(g + 1) * SSM_D_STATE]
        cg = cm[:, g * SSM_D_STATE:(g + 1) * SSM_D_STATE]
        cb = _dot_nt(cg, bg)
        rows = slice(g * SSM_GROUP_WIDTH, (g + 1) * SSM_GROUP_WIDTH)
        s_in = st_ref[rows, :]
        y_off = _dot_nt(cg, s_in.astype(bf16)) * ea_f[:, rows]
        diag = []
        for hh in range(hpg):
            h = g * hpg + hh
            seg = acum[:, h:h + 1] - acum_t[h:h + 1, :]
            dec = jnp.exp(jnp.where(causal, seg, -jnp.inf))
            m = (cb * dec).astype(bf16)
            diag.append(_dot(m, xdt_b[:, h * SSM_HEADDIM:(h + 1) * SSM_HEADDIM]))
        y_parts.append(jnp.concatenate(diag, axis=1) + y_off)
        st_new = _dot_tn(xdtw_b[:, rows], bg)
        st_ref[rows, :] = s_in * cd_rows[rows, :] + st_new
    y = jnp.concatenate(y_parts, axis=1) + dsk_ref[...] * xs
    yz = y * _silu(z_ref[...])
    outs = []
    for g in range(SSM_GROUPS):
        outs.append(_rms(yz[:, g * SSM_GROUP_WIDTH:(g + 1) * SSM_GROUP_WIDTH]))
    y_ref[...] = (jnp.concatenate(outs, axis=1) * ng_ref[...]).astype(bf16)


def _ssd_prompt(xbc, z, dt, cw, cb, dtb, alog, dsk, ng, e, et, layer, T):
    nb, L, _ = xbc.shape
    tok = lambda w: pl.BlockSpec((None, T, w), lambda b, i: (b, i, 0))
    par = lambda r, w: pl.BlockSpec((None, r, w), lambda b, i: (layer, 0, 0))
    return pl.pallas_call(
        functools.partial(_ssd_kernel, T=T),
        grid=(nb, L // T),
        in_specs=[tok(SSM_CONV_DIM), tok(SSM_D_INNER), tok(DT_PAD),
                  par(SSM_CONV, SSM_CONV_DIM), par(1, SSM_CONV_DIM), par(1, DT_PAD), par(1, DT_PAD),
                  par(1, SSM_D_INNER), par(1, SSM_D_INNER),
                  pl.BlockSpec((LANES, SSM_D_INNER), lambda b, i: (0, 0)),
                  pl.BlockSpec((SSM_D_INNER, LANES), lambda b, i: (0, 0))],
        out_specs=[tok(SSM_D_INNER),
                   pl.BlockSpec((None, SSM_D_INNER, SSM_D_STATE), lambda b, i: (b, 0, 0))],
        out_shape=[jax.ShapeDtypeStruct((nb, L, SSM_D_INNER), bf16),
                   jax.ShapeDtypeStruct((nb, SSM_D_INNER, SSM_D_STATE), f32)],
        scratch_shapes=[pltpu.VMEM((T + SUBLANES, SSM_CONV_DIM), f32)],
        compiler_params=_cparams(("parallel", "arbitrary")),
        name="ssd_prompt",
    )(xbc, z, dt, cw, cb, dtb, alog, dsk, ng, e, et)


def _lambda_value(lam_ref, lam_init):
    lv = lam_ref[...]
    a = jnp.sum(lv[0:1, :] * lv[1:2, :], axis=-1, keepdims=True)
    b = jnp.sum(lv[2:3, :] * lv[3:4, :], axis=-1, keepdims=True)
    return jnp.exp(a) - jnp.exp(b) + lam_init


def _attn_kernel(qi_ref, ki_ref, q_ref, k_ref, v_ref, slope_ref, lam_ref, sg_ref, o_ref,
                 m_ref, l_ref, acc_ref, *, tq, tk, lam_init):
    g = pl.program_id(1)
    p = pl.program_id(2)
    qi = qi_ref[p]
    ki = ki_ref[p]

    @pl.when(ki == 0)
    def _():
        m_ref[...] = jnp.full_like(m_ref, -jnp.inf)
        l_ref[...] = jnp.zeros_like(l_ref)
        acc_ref[...] = jnp.zeros_like(acc_ref)

    q = q_ref[...]
    k = k_ref[...]
    v = v_ref[...]
    row = lax.broadcasted_iota(jnp.int32, (tq, tk), 0) + qi * tq
    col = lax.broadcasted_iota(jnp.int32, (tq, tk), 1) + ki * tk
    dist = (row - col).astype(f32)
    causal = row >= col
    scale = ATT_QK_DIM ** -0.5
    for r in range(ATT_GROUP):
        bias = dist * (-slope_ref[g, r])
        for c in range(2):
            idx = r * 2 + c
            s = _dot_nt(q[:, idx * ATT_QK_DIM:(idx + 1) * ATT_QK_DIM],
                        k[:, c * ATT_QK_DIM:(c + 1) * ATT_QK_DIM]) * scale
            s = jnp.where(causal, s + bias, -jnp.inf)
            m_old = m_ref[idx]
            m_new = jnp.maximum(m_old, jnp.max(s, axis=1, keepdims=True))
            alpha = jnp.exp(m_old - m_new)
            pe = jnp.exp(s - m_new)
            l_ref[idx] = alpha * l_ref[idx] + jnp.sum(pe, axis=1, keepdims=True)
            acc_ref[idx] = alpha * acc_ref[idx] + _dot(pe.astype(bf16), v)
            m_ref[idx] = m_new

    @pl.when(ki == qi)
    def _():
        lam = _lambda_value(lam_ref, lam_init)
        for r in range(ATT_GROUP):
            o = acc_ref[2 * r] / l_ref[2 * r] - lam * (acc_ref[2 * r + 1] / l_ref[2 * r + 1])
            o = _rms(o) * sg_ref[...] * (1.0 - lam_init)
            o_ref[:, r * ATT_V_DIM:(r + 1) * ATT_V_DIM] = o.astype(bf16)


def _attn_prompt(q, kb, vb, slopes, lam_vecs, subln_g, layer, lam_init, tq):
    nb, L, _ = q.shape
    nq = L // tq
    pairs = [(i, j) for i in range(nq) for j in range(i + 1)]
    qi_tab = jnp.asarray(np.array([a for a, _ in pairs], np.int32))
    ki_tab = jnp.asarray(np.array([b for _, b in pairs], np.int32))
    qw = ATT_GROUP * 2 * ATT_QK_DIM
    kw = 2 * ATT_QK_DIM
    ow = ATT_GROUP * ATT_V_DIM
    grid_spec = pltpu.PrefetchScalarGridSpec(
        num_scalar_prefetch=2,
        grid=(nb, ATT_KV_HEADS, len(pairs)),
        in_specs=[
            pl.BlockSpec((None, tq, qw), lambda b, g, p, qi, ki: (b, qi[p], g)),
            pl.BlockSpec((None, tq, kw), lambda b, g, p, qi, ki: (b, ki[p], g)),
            pl.BlockSpec((None, tq, ATT_V_DIM), lambda b, g, p, qi, ki: (b, ki[p], g)),
            pl.BlockSpec(memory_space=pltpu.SMEM),
            pl.BlockSpec((None, 4, ATT_QK_DIM), lambda b, g, p, qi, ki: (layer, 0, 0)),
            pl.BlockSpec((None, 1, ATT_V_DIM), lambda b, g, p, qi, ki: (layer, 0, 0)),
        ],
        out_specs=pl.BlockSpec((None, tq, ow), lambda b, g, p, qi, ki: (b, qi[p], g)),
        scratch_shapes=[pltpu.VMEM((4, tq, 1), f32), pltpu.VMEM((4, tq, 1), f32),
                        pltpu.VMEM((4, tq, ATT_V_DIM), f32)],
    )
    return pl.pallas_call(
        functools.partial(_attn_kernel, tq=tq, tk=tq, lam_init=lam_init),
        grid_spec=grid_spec,
        out_shape=jax.ShapeDtypeStruct((nb, L, ATT_WIDTH), bf16),
        compiler_params=_cparams(("parallel", "parallel", "arbitrary")),
        name="attn_prompt",
    )(qi_tab, ki_tab, q, kb, vb, slopes, lam_vecs, subln_g)


def _merge_kernel(ys_ref, ya_ref, gt_ref, x_ref, mod_ref, g_ref, wb_ref, wo_ref, o_ref):
    bp0 = _dot(ys_ref[...], wb_ref[0])
    bp1 = _dot(ya_ref[...], wb_ref[1])
    gt = gt_ref[...]
    merged = (jax.nn.sigmoid(gt[:, :D_MODEL]) * bp0 + jax.nn.sigmoid(gt[:, D_MODEL:]) * bp1)
    m = _dot(merged.astype(bf16), wo_ref[...])
    o_ref[...] = x_ref[...] + mod_ref[2] * (_rms(m) * g_ref[...])


def _merge(y_ssm, y_att, gates, x, mod, g_post, wb, wo, layer, tl):
    nb, L, _ = x.shape
    r = mod.shape[2]
    tok = lambda w: pl.BlockSpec((None, tl, w), lambda b, i: (b, i, 0))
    return pl.pallas_call(
        _merge_kernel,
        grid=(nb, L // tl),
        in_specs=[
            tok(SSM_D_INNER), tok(ATT_WIDTH), tok(2 * D_MODEL), tok(D_MODEL),
            pl.BlockSpec((6, None, r, D_MODEL), lambda b, i: (0, b, 0, 0)),
            pl.BlockSpec((None, 1, D_MODEL), lambda b, i: (layer, 0, 0)),
            pl.BlockSpec((None, 2, ATT_WIDTH, D_MODEL), lambda b, i: (layer, 0, 0, 0),
                         pipeline_mode=pl.Buffered(1)),
            pl.BlockSpec((None, D_MODEL, D_MODEL), lambda b, i: (layer, 0, 0),
                         pipeline_mode=pl.Buffered(1)),
        ],
        out_specs=tok(D_MODEL),
        out_shape=jax.ShapeDtypeStruct((nb, L, D_MODEL), f32),
        compiler_params=_cparams(("parallel", "parallel")),
        name="merge_out",
    )(y_ssm, y_att, gates, x, mod, g_post, wb, wo)


def _ffn_kernel(x_ref, mod_ref, g1_ref, g2_ref, wgu_ref, wd_ref, o_ref):
    x = x_ref[...]
    h = _rms(x) * g1_ref[...]
    hb = (h * (1.0 + mod_ref[4]) + mod_ref[3]).astype(bf16)
    gg = _dot(hb, wgu_ref[:, :FFN_HIDDEN])
    uu = _dot(hb, wgu_ref[:, FFN_HIDDEN:])
    f = _dot((_silu(gg) * uu).astype(bf16), wd_ref[...])
    o_ref[...] = x + mod_ref[5] * (_rms(f) * g2_ref[...])


def _ffn(x, mod, g_pre, g_post, wgu, wd, layer, tl):
    nb, L, _ = x.shape
    r = mod.shape[2]
    tok = pl.BlockSpec((None, tl, D_MODEL), lambda b, i: (b, i, 0))
    par = pl.BlockSpec((None, 1, D_MODEL), lambda b, i: (layer, 0, 0))
    return pl.pallas_call(
        _ffn_kernel,
        grid=(nb, L // tl),
        in_specs=[
            tok,
            pl.BlockSpec((6, None, r, D_MODEL), lambda b, i: (0, b, 0, 0)),
            par, par,
            pl.BlockSpec((None, D_MODEL, 2 * FFN_HIDDEN), lambda b, i: (layer, 0, 0),
                         pipeline_mode=pl.Buffered(1)),
            pl.BlockSpec((None, FFN_HIDDEN, D_MODEL), lambda b, i: (layer, 0, 0),
                         pipeline_mode=pl.Buffered(1)),
        ],
        out_specs=tok,
        out_shape=jax.ShapeDtypeStruct((nb, L, D_MODEL), f32),
        compiler_params=_cparams(("parallel", "parallel")),
        name="ffn",
    )(x, mod, g_pre, g_post, wgu, wd)


def _sample_pre_kernel(xbc_ref, sc_ref, dt_ref, cw_ref, cb_ref, dtb_ref, alog_ref, e_ref,
                       xs_ref, bc_ref, dec_ref, xdt_ref, cn_ref):
    xr = xbc_ref[...]
    acc = xr * cw_ref[SSM_CONV - 1:SSM_CONV, :]
    for j in range(SSM_CONV - 1):
        acc = acc + sc_ref[j] * cw_ref[j:j + 1, :]
    conv = _silu(acc + cb_ref[...])
    xs = conv[:, :SSM_D_INNER]
    xs_ref[...] = xs
    bc_ref[...] = conv[:, SSM_D_INNER:]
    for j in range(SSM_CONV - 2):
        cn_ref[j] = sc_ref[j + 1]
    cn_ref[SSM_CONV - 2] = xr
    dt = _softplus(dt_ref[...] + dtb_ref[...])
    da = dt * (-jnp.exp(alog_ref[...]))
    e = e_ref[...]
    dec = jnp.exp(_dot_split(da, e, 3))
    xdt = xs * _dot_split(dt, e, 3)
    for j, p in enumerate(_split_bf16(dec, 3)):
        dec_ref[j] = p
    for j, p in enumerate(_split_bf16(xdt, 2)):
        xdt_ref[j] = p


def _sample_pre(xbc, sc_t, dt, cw, cb, dtb, alog, e, layer):
    n = xbc.shape[0]
    full = lambda *s: pl.BlockSpec(s, lambda i: (0,) * len(s))
    par = lambda r, w: pl.BlockSpec((None, r, w), lambda i: (layer, 0, 0))
    return pl.pallas_call(
        _sample_pre_kernel,
        grid=(1,),
        in_specs=[full(n, SSM_CONV_DIM), full(SSM_CONV - 1, n, SSM_CONV_DIM), full(n, DT_PAD),
                  par(SSM_CONV, SSM_CONV_DIM), par(1, SSM_CONV_DIM), par(1, DT_PAD), par(1, DT_PAD),
                  full(LANES, SSM_D_INNER)],
        out_specs=[full(n, SSM_D_INNER), full(n, 2 * SSM_BC), full(3, n, SSM_D_INNER),
                   full(2, n, SSM_D_INNER), full(SSM_CONV - 1, n, SSM_CONV_DIM)],
        out_shape=[jax.ShapeDtypeStruct((n, SSM_D_INNER), f32),
                   jax.ShapeDtypeStruct((n, 2 * SSM_BC), f32),
                   jax.ShapeDtypeStruct((3, n, SSM_D_INNER), bf16),
                   jax.ShapeDtypeStruct((2, n, SSM_D_INNER), bf16),
                   jax.ShapeDtypeStruct((SSM_CONV - 1, n, SSM_CONV_DIM), f32)],
        compiler_params=_cparams(("arbitrary",)),
        name="sample_pre",
    )(xbc, sc_t, dt, cw, cb, dtb, alog, e)


def _sample_state_kernel(s_ref, dec_ref, xdt_ref, bc_ref, so_ref, y_ref, *, n):
    b = pl.program_id(0)

    def onehot(copies):
        rr = lax.broadcasted_iota(jnp.int32, (copies * n, SSM_D_STATE), 0)
        hit = rr == b
        for c in range(1, copies):
            hit = jnp.logical_or(hit, rr == b + c * n)
        return hit.astype(bf16)

    dm = _dot(dec_ref[...], onehot(3))
    xm = _dot(xdt_ref[...], onehot(2))
    brow = bc_ref[pl.ds(b, 1), :]
    y_parts = []
    for g in range(SSM_GROUPS):
        rows = slice(g * SSM_GROUP_WIDTH, (g + 1) * SSM_GROUP_WIDTH)
        bg = brow[:, g * SSM_D_STATE:(g + 1) * SSM_D_STATE]
        cg = brow[:, SSM_BC + g * SSM_D_STATE:SSM_BC + (g + 1) * SSM_D_STATE]
        s_new = s_ref[rows, :] * dm[rows, :] + xm[rows, :] * bg
        so_ref[rows, :] = s_new
        c8 = jnp.broadcast_to(cg, (SUBLANES, SSM_D_STATE)).astype(bf16)
        y_parts.append(_dot_nt(c8, s_new.astype(bf16))[0:1, :])
    y_ref[...] = jnp.concatenate(y_parts, axis=1)


def _sample_state(state, dec_t, xdt_t, bc, layer):
    n = bc.shape[0]
    return pl.pallas_call(
        functools.partial(_sample_state_kernel, n=n),
        grid=(n,),
        in_specs=[
            pl.BlockSpec((None, None, SSM_D_INNER, SSM_D_STATE), lambda b: (layer, b, 0, 0)),
            pl.BlockSpec((SSM_D_INNER, 3 * n), lambda b: (0, 0)),
            pl.BlockSpec((SSM_D_INNER, 2 * n), lambda b: (0, 0)),
            pl.BlockSpec((n, 2 * SSM_BC), lambda b: (0, 0)),
        ],
        out_specs=[pl.BlockSpec((None, SSM_D_INNER, SSM_D_STATE), lambda b: (b, 0, 0)),
                   pl.BlockSpec((None, 1, SSM_D_INNER), lambda b: (b, 0, 0))],
        out_shape=[jax.ShapeDtypeStruct((n, SSM_D_INNER, SSM_D_STATE), f32),
                   jax.ShapeDtypeStruct((n, 1, SSM_D_INNER), f32)],
        compiler_params=_cparams(("parallel",)),
        name="sample_state",
    )(state, dec_t, xdt_t, bc)


def _sample_gate_kernel(y_ref, xs_ref, z_ref, dsk_ref, ng_ref, o_ref):
    y = y_ref[...] + dsk_ref[...] * xs_ref[...]
    yz = y * _silu(z_ref[...])
    outs = []
    for g in range(SSM_GROUPS):
        outs.append(_rms(yz[:, g * SSM_GROUP_WIDTH:(g + 1) * SSM_GROUP_WIDTH]))
    o_ref[...] = (jnp.concatenate(outs, axis=1) * ng_ref[...]).astype(bf16)


def _sample_gate(y, xs, z, dsk, ng, layer):
    n = y.shape[0]
    full = pl.BlockSpec((n, SSM_D_INNER), lambda i: (0, 0))
    par = pl.BlockSpec((None, 1, SSM_D_INNER), lambda i: (layer, 0, 0))
    return pl.pallas_call(
        _sample_gate_kernel,
        grid=(1,),
        in_specs=[full, full, full, par, par],
        out_specs=full,
        out_shape=jax.ShapeDtypeStruct((n, SSM_D_INNER), bf16),
        compiler_params=_cparams(("arbitrary",)),
        name="sample_gate",
    )(y, xs, z, dsk, ng)


def _paged_attn_kernel(pt_ref, qm_ref, kn_ref, vn_ref, ns_ref, lam_ref, sg_ref, *rest, pps,
                       n_steps, past, lam_init):
    del pt_ref
    k_refs = rest[:pps]
    v_refs = rest[pps:2 * pps]
    o_ref = rest[2 * pps]
    m_ref, l_ref, acc_ref, s_scr = rest[2 * pps + 1:]
    j = pl.program_id(1)
    qm = qm_ref[...]
    nslope = ns_ref[...]
    scale = ATT_QK_DIM ** -0.5

    @pl.when(j == 0)
    def _():
        s0 = _dot(kn_ref[...], qm) * scale
        first = lax.broadcasted_iota(jnp.int32, s0.shape, 0) == 0
        s0 = jnp.where(first, s0, -jnp.inf)
        m0 = jnp.max(s0, axis=0, keepdims=True)
        p0 = jnp.exp(s0 - m0)
        m_ref[...] = m0
        l_ref[...] = jnp.sum(p0, axis=0, keepdims=True)
        acc_ref[...] = _dot_tn(vn_ref[...], p0.astype(bf16))

    base = j * (pps * PAGE_SIZE)
    key = lax.broadcasted_iota(jnp.int32, (PAGE_SIZE, LANES), 0)
    for i in range(pps):
        s = _dot(k_refs[i][...].astype(bf16), qm) * scale
        dist = (past - (base + i * PAGE_SIZE) - key).astype(f32)
        s_scr[i * PAGE_SIZE:(i + 1) * PAGE_SIZE, :] = s + dist * nslope
    s_all = s_scr[...]
    m_old = m_ref[...]
    m_new = jnp.maximum(m_old, jnp.max(s_all, axis=0, keepdims=True))
    alpha = jnp.exp(m_old - m_new)
    pe = jnp.exp(s_all - m_new)
    l_ref[...] = alpha * l_ref[...] + jnp.sum(pe, axis=0, keepdims=True)
    pb = pe.astype(bf16)
    pv = None
    for i in range(pps):
        t = _dot_tn(v_refs[i][...].astype(bf16), pb[i * PAGE_SIZE:(i + 1) * PAGE_SIZE, :])
        pv = t if pv is None else pv + t
    acc_ref[...] = acc_ref[...] * alpha + pv
    m_ref[...] = m_new

    @pl.when(j == n_steps - 1)
    def _():
        lam = _lambda_value(lam_ref, lam_init)
        inv_l = 1.0 / l_ref[...]
        for g in range(ATT_KV_HEADS):
            ot = (acc_ref[g * ATT_V_DIM:(g + 1) * ATT_V_DIM, :] * inv_l).T
            for r in range(ATT_GROUP):
                c0 = g * (ATT_GROUP * 2) + r * 2
                o = ot[c0:c0 + 1, :] - lam * ot[c0 + 1:c0 + 2, :]
                o = _rms(o) * sg_ref[...] * (1.0 - lam_init)
                h = g * ATT_GROUP + r
                o_ref[:, h * ATT_V_DIM:(h + 1) * ATT_V_DIM] = o.astype(bf16)


def _paged_attn(page_table, qmat, k_new, v_new, nslope, lam_vecs, subln_g, cache_k, cache_v,
                layer, lam_init, pps):
    n, n_pages = page_table.shape
    n_steps = n_pages // pps
    past = n_pages * PAGE_SIZE

    def page_spec(i):
        return pl.BlockSpec((None, None, PAGE_SIZE, ATT_K_COLS),
                            lambda b, j, pt: (layer, pt[b, j * pps + i], 0, 0))

    grid_spec = pltpu.PrefetchScalarGridSpec(
        num_scalar_prefetch=1,
        grid=(n, n_steps),
        in_specs=[
            pl.BlockSpec((None, ATT_K_COLS, LANES), lambda b, j, pt: (b, 0, 0)),
            pl.BlockSpec((None, SUBLANES, ATT_K_COLS), lambda b, j, pt: (b, 0, 0)),
            pl.BlockSpec((None, SUBLANES, ATT_V_COLS), lambda b, j, pt: (b, 0, 0)),
            pl.BlockSpec((1, LANES), lambda b, j, pt: (0, 0)),
            pl.BlockSpec((None, 4, ATT_QK_DIM), lambda b, j, pt: (layer, 0, 0)),
            pl.BlockSpec((None, 1, ATT_V_DIM), lambda b, j, pt: (layer, 0, 0)),
        ] + [page_spec(i) for i in range(pps)] + [page_spec(i) for i in range(pps)],
        out_specs=pl.BlockSpec((None, 1, ATT_WIDTH), lambda b, j, pt: (b, 0, 0)),
        scratch_shapes=[pltpu.VMEM((1, LANES), f32), pltpu.VMEM((1, LANES), f32),
                        pltpu.VMEM((ATT_V_COLS, LANES), f32),
                        pltpu.VMEM((pps * PAGE_SIZE, LANES), f32)],
    )
    return pl.pallas_call(
        functools.partial(_paged_attn_kernel, pps=pps, n_steps=n_steps, past=past,
                          lam_init=lam_init),
        grid_spec=grid_spec,
        out_shape=jax.ShapeDtypeStruct((n, 1, ATT_WIDTH), bf16),
        compiler_params=_cparams(("parallel", "arbitrary")),
        name="paged_attn",
    )(page_table, qmat, k_new, v_new, nslope, lam_vecs, subln_g,
      *([cache_k] * pps), *([cache_v] * pps))


def _alibi_tables():
    h = np.arange(1, ATT_HEADS + 1, dtype=np.float32)
    slopes = (2.0 ** (-8.0 * h / ATT_HEADS)).astype(np.float32).reshape(ATT_KV_HEADS, ATT_GROUP)
    per_col = np.zeros((1, LANES), np.float32)
    per_col[0, :ATT_SCORE_COLS] = -np.repeat(slopes.reshape(-1), 2)
    return slopes, per_col


def _expand_matrix():
    e = np.zeros((LANES, SSM_D_INNER), np.float32)
    for h in range(SSM_HEADS):
        e[h, h * SSM_HEADDIM:(h + 1) * SSM_HEADDIM] = 1.0
    return e


def _pick(n, prefer):
    t = min(prefer, n)
    while n % t:
        t //= 2
    return t


def _query_matrix(q):
    n = q.shape[0]
    q5 = q.reshape(n, ATT_KV_HEADS, ATT_GROUP, 2, ATT_QK_DIM)
    eg = jnp.eye(ATT_KV_HEADS, dtype=q.dtype)
    ec = jnp.eye(2, dtype=q.dtype)
    qm = jnp.einsum('bgrcd,gh,ce->bgcdhre', q5, eg, ec)
    qm = qm.reshape(n, ATT_K_COLS, ATT_SCORE_COLS)
    return jnp.pad(qm, ((0, 0), (0, 0), (0, LANES - ATT_SCORE_COLS)))


def kernel(x_prompt, x_sample, cache_k, cache_v, state_conv, state_ssm, page_table, c_prompt,
           c_sample, w_ada, b_ada, g_pre_mix, g_post_mix, g_pre_ffn, g_post_ffn, w_in, conv_w,
           conv_b, dt_bias, a_log, d_skip, ssm_norm_g, lam_vecs, subln_g, w_branch, w_out, w_gu,
           w_down):
    depth = w_in.shape[0]
    bp, L, _ = x_prompt.shape
    n = x_sample.shape[0]
    n_pool = cache_k.shape[1]
    n_pages = page_table.shape[1]

    z_end = SSM_D_INNER
    xbc_end = z_end + SSM_CONV_DIM
    dt_end = xbc_end + SSM_HEADS
    q_end = dt_end + ATT_Q_COLS
    k_end = q_end + ATT_K_COLS
    v_end = k_end + ATT_V_COLS
    w_dt = jnp.pad(w_in[:, :, xbc_end:dt_end], ((0, 0), (0, 0), (0, DT_PAD - SSM_HEADS)))
    w_cat = jnp.concatenate(
        [w_in[:, :, :xbc_end], w_in[:, :, dt_end:], w_dt], axis=-1).astype(bf16)
    w_ada_b = w_ada.astype(bf16)
    wb_b = w_branch.astype(bf16)
    wo_b = w_out.astype(bf16)
    wgu_b = w_gu.astype(bf16)
    wd_b = w_down.astype(bf16)
    row = lambda a: a.reshape(depth, 1, a.shape[-1])
    pad_heads = lambda a: jnp.pad(a, ((0, 0), (0, DT_PAD - SSM_HEADS))).reshape(depth, 1, DT_PAD)
    dtb_p = pad_heads(dt_bias)
    alog_p = pad_heads(a_log)
    dsk_f = jnp.repeat(d_skip, SSM_HEADDIM, axis=-1).reshape(depth, 1, SSM_D_INNER)
    g_pre_mix, g_post_mix, g_pre_ffn, g_post_ffn = map(row, (g_pre_mix, g_post_mix, g_pre_ffn,
                                                            g_post_ffn))
    cb_r, ng_r, sg_r = row(conv_b), row(ssm_norm_g), row(subln_g)
    slopes_np, nslope_np = _alibi_tables()
    slopes = jnp.asarray(slopes_np)
    nslope = jnp.asarray(nslope_np)
    e_np = _expand_matrix()
    e_mat = jnp.asarray(e_np, dtype=bf16)
    et_mat = jnp.asarray(e_np.T, dtype=bf16)

    cache_k4 = cache_k.reshape(depth, n_pool, PAGE_SIZE, ATT_K_COLS)
    cache_v4 = cache_v.reshape(depth, n_pool, PAGE_SIZE, ATT_V_COLS)
    state4 = state_ssm.reshape(depth, n, SSM_D_INNER, SSM_D_STATE)

    mod_all = _ada_mod(jnp.concatenate([c_prompt, c_sample], axis=0), w_ada_b, b_ada)

    tl_p = _pick(L, 256)
    t_ssd = _pick(L, 128)
    tq = _pick(L, 512)
    pps = _pick(n_pages, 8)

    xp = x_prompt
    xs = x_sample.reshape(1, n, D_MODEL)
    outs = [[] for _ in range(8)]
    for l in range(depth):
        lam_init = 0.8 - 0.6 * math.exp(-0.3 * l)
        mod_p = mod_all[l, :, :bp].reshape(6, bp, 1, D_MODEL)
        mod_s = mod_all[l, :, bp:].reshape(6, 1, n, D_MODEL)

        z, xbc, q, k, v, kb, vb, gates, dt = _in_proj(xp, mod_p, g_pre_mix, w_cat, l, tl_p)
        y_ssm, st = _ssd_prompt(xbc, z, dt, conv_w, cb_r, dtb_p, alog_p, dsk_f, ng_r, e_mat,
                                et_mat, l, t_ssd)
        y_att = _attn_prompt(q, kb, vb, slopes, lam_vecs, sg_r, l, lam_init, tq)
        x1 = _merge(y_ssm, y_att, gates, xp, mod_p, g_post_mix, wb_b, wo_b, l, tl_p)
        xp = _ffn(x1, mod_p, g_pre_ffn, g_post_ffn, wgu_b, wd_b, l, tl_p)
        outs[0].append(k.reshape(bp, L, ATT_KV_HEADS, 2, ATT_QK_DIM))
        outs[1].append(v.reshape(bp, L, ATT_KV_HEADS, ATT_V_DIM))
        outs[2].append(xbc[:, L - (SSM_CONV - 1):, :])
        outs[3].append(st.reshape(bp, SSM_HEADS, SSM_HEADDIM, SSM_D_STATE))

        z, xbc, q, k, v, kb, vb, gates, dt = _in_proj(xs, mod_s, g_pre_mix, w_cat, l, n)
        sc_t = jnp.transpose(state_conv[l], (1, 0, 2))
        xs_c, bc, dec3, xdt2, conv_new = _sample_pre(xbc[0], sc_t, dt[0], conv_w, cb_r, dtb_p,
                                                     alog_p, e_mat, l)
        dec_t = jnp.transpose(dec3, (2, 0, 1)).reshape(SSM_D_INNER, 3 * n)
        xdt_t = jnp.transpose(xdt2, (2, 0, 1)).reshape(SSM_D_INNER, 2 * n)
        st_s, y_raw = _sample_state(state4, dec_t, xdt_t, bc, l)
        y_ssm = _sample_gate(y_raw.reshape(n, SSM_D_INNER), xs_c, z[0], dsk_f, ng_r, l)
        qmat = _query_matrix(q[0])
        first_row = lambda a: jnp.pad(a[0][:, None, :], ((0, 0), (0, SUBLANES - 1), (0, 0)))
        y_att = _paged_attn(page_table, qmat, first_row(kb), first_row(vb), nslope, lam_vecs, sg_r,
                            cache_k4, cache_v4, l, lam_init, pps)
        x1 = _merge(y_ssm.reshape(1, n, SSM_D_INNER), y_att.reshape(1, n, ATT_WIDTH), gates, xs,
                    mod_s, g_post_mix, wb_b, wo_b, l, n)
        xs = _ffn(x1, mod_s, g_pre_ffn, g_post_ffn, wgu_b, wd_b, l, n)
        outs[4].append(k.reshape(n, 1, ATT_KV_HEADS, 2, ATT_QK_DIM))
        outs[5].append(v.reshape(n, 1, ATT_KV_HEADS, ATT_V_DIM))
        outs[6].append(jnp.transpose(conv_new, (1, 0, 2)))
        outs[7].append(st_s.reshape(n, SSM_HEADS, SSM_HEADDIM, SSM_D_STATE))

    stacked = [jnp.stack(o) for o in outs]
    return (xp, xs.reshape(n, 1, D_MODEL), *stacked)
```

```python
import functools
import math

import numpy as np
import jax
import jax.numpy as jnp
from jax import lax
from jax.experimental import pallas as pl
from jax.experimental.pallas import tpu as pltpu

D_MODEL = 1024
SSM_HEADDIM = 64
SSM_HEADS = 16
SSM_GROUPS = 2
SSM_D_STATE = 128
SSM_CONV = 4
SSM_D_INNER = SSM_HEADS * SSM_HEADDIM
SSM_BC = SSM_GROUPS * SSM_D_STATE
SSM_CONV_DIM = SSM_D_INNER + 2 * SSM_BC
SSM_GROUP_WIDTH = SSM_D_INNER // SSM_GROUPS
ATT_HEADS = 8
ATT_KV_HEADS = 4
ATT_GROUP = ATT_HEADS // ATT_KV_HEADS
ATT_QK_DIM = 64
ATT_V_DIM = 128
ATT_WIDTH = ATT_HEADS * ATT_V_DIM
ATT_Q_COLS = ATT_HEADS * 2 * ATT_QK_DIM
ATT_K_COLS = ATT_KV_HEADS * 2 * ATT_QK_DIM
ATT_V_COLS = ATT_KV_HEADS * ATT_V_DIM
ATT_SCORE_COLS = ATT_KV_HEADS * ATT_GROUP * 2
FFN_HIDDEN = 2816
RMS_EPS = 1e-6
PAGE_SIZE = 128
LANES = 128
SUBLANES = 8
DT_PAD = LANES

_SEG_Z = (0, SSM_D_INNER)
_SEG_XBC = (_SEG_Z[1], _SEG_Z[1] + SSM_CONV_DIM)
_SEG_Q = (_SEG_XBC[1], _SEG_XBC[1] + ATT_Q_COLS)
_SEG_K = (_SEG_Q[1], _SEG_Q[1] + ATT_K_COLS)
_SEG_V = (_SEG_K[1], _SEG_K[1] + ATT_V_COLS)
_SEG_G = (_SEG_V[1], _SEG_V[1] + 2 * D_MODEL)
_SEG_DT = (_SEG_G[1], _SEG_G[1] + DT_PAD)
ATT_KW_COLS = ATT_KV_HEADS * 2 * LANES
_SEG_KW = (_SEG_DT[1], _SEG_DT[1] + ATT_KW_COLS)
IN_COLS_PAD = _SEG_KW[1]
ONES_ROWS = 2 * SUBLANES
POS_TERMS = 3
LOG2E = math.log2(math.e)

VMEM_LIMIT = 56 * 1024 * 1024

f32 = jnp.float32
bf16 = jnp.bfloat16


def _cparams(sem):
    return pltpu.CompilerParams(dimension_semantics=sem, vmem_limit_bytes=VMEM_LIMIT)


def _rms(x):
    return x * lax.rsqrt(jnp.mean(x * x, axis=-1, keepdims=True) + RMS_EPS)


def _silu(x):
    return x * jax.nn.sigmoid(x)


def _softplus(x):
    return jnp.maximum(x, 0.0) + jnp.log(1.0 + jnp.exp(-jnp.abs(x)))


def _split_bf16(x, terms):
    parts = []
    rem = x
    for _ in range(terms):
        p = rem.astype(bf16)
        parts.append(p)
        rem = rem - p.astype(f32)
    return parts


def _dot(a, b):
    return jnp.dot(a, b, preferred_element_type=f32)


def _dot_nt(a, b):
    return lax.dot_general(a, b, (((1,), (1,)), ((), ())), preferred_element_type=f32)


def _dot_tn(a, b):
    return lax.dot_general(a, b, (((0,), (0,)), ((), ())), preferred_element_type=f32)


def _dot_split(x, w, terms):
    acc = None
    for p in _split_bf16(x, terms):
        t = _dot(p, w)
        acc = t if acc is None else acc + t
    return acc


def _ada_kernel(c_ref, w_ref, b_ref, o_ref):
    s = _silu(c_ref[...])
    o_ref[...] = _dot(s.astype(bf16), w_ref[...]) + b_ref[...]


def _ada_mod(c_all, w_ada_b, b_ada):
    depth = w_ada_b.shape[0]
    rows = c_all.shape[0]
    return pl.pallas_call(
        _ada_kernel,
        grid=(depth, 6),
        in_specs=[
            pl.BlockSpec((rows, D_MODEL), lambda l, j: (0, 0)),
            pl.BlockSpec((None, D_MODEL, D_MODEL), lambda l, j: (l, 0, j)),
            pl.BlockSpec((None, 1, D_MODEL), lambda l, j: (l, 0, j)),
        ],
        out_specs=pl.BlockSpec((None, None, rows, D_MODEL), lambda l, j: (l, j, 0, 0)),
        out_shape=jax.ShapeDtypeStruct((depth, 6, rows, D_MODEL), f32),
        compiler_params=_cparams(("arbitrary", "arbitrary")),
        name="ada_mod",
    )(c_all, w_ada_b, b_ada.reshape(depth, 1, 6 * D_MODEL))


def _inproj_kernel(x_ref, mod_ref, g_ref, w_ref, z_ref, xbc_ref, q_ref, k_ref, v_ref, kb_ref,
                   vb_ref, gt_ref, dt_ref):
    h = _rms(x_ref[...]) * g_ref[...]
    h = h * (1.0 + mod_ref[1]) + mod_ref[0]
    hb = h.astype(bf16)

    def seg(lo_hi):
        return _dot(hb, w_ref[:, lo_hi[0]:lo_hi[1]])

    z_ref[...] = seg(_SEG_Z)
    xbc_ref[...] = seg(_SEG_XBC)
    q_ref[...] = seg(_SEG_Q).astype(bf16)
    k_ref[...] = seg(_SEG_K)
    rows = x_ref.shape[0]
    lm = lax.broadcasted_iota(jnp.int32, (rows, ATT_KW_COLS), 1) & (2 * LANES - 1)
    spare = (((lm >= ATT_QK_DIM) & (lm < ATT_QK_DIM + 2 * POS_TERMS))
             | ((lm >= LANES) & (lm < LANES + 2 * POS_TERMS)))
    pos = pl.program_id(1) * rows + lax.broadcasted_iota(jnp.int32, (rows, ATT_KW_COLS), 0)
    hi = ((pos >> 8) << 8).astype(f32)
    lo = (pos & 255).astype(f32)
    extra = jnp.where(spare, jnp.where((lm & 1) == 0, hi, lo), 0.0)
    kb_ref[...] = (seg(_SEG_KW) + extra).astype(bf16)
    vv = seg(_SEG_V)
    v_ref[...] = vv
    vb_ref[...] = vv.T.astype(bf16)
    gt_ref[...] = seg(_SEG_G)
    dt_ref[...] = seg(_SEG_DT)


def _in_proj(x, mod, g_pre, w_cat, layer, tl):
    nb, L, _ = x.shape
    r = mod.shape[2]
    widths = [(SSM_D_INNER, f32), (SSM_CONV_DIM, f32), (ATT_Q_COLS, bf16), (ATT_K_COLS, f32),
              (ATT_V_COLS, f32), (ATT_KW_COLS, bf16), (ATT_V_COLS, bf16), (2 * D_MODEL, f32),
              (DT_PAD, f32)]
    tok = lambda w: pl.BlockSpec((None, tl, w), lambda b, i: (b, i, 0))
    vt_index = 6
    out_specs = [tok(w) for w, _ in widths]
    out_shape = [jax.ShapeDtypeStruct((nb, L, w), dt) for w, dt in widths]
    out_specs[vt_index] = pl.BlockSpec((None, ATT_V_COLS, tl), lambda b, i: (b, 0, i))
    out_shape[vt_index] = jax.ShapeDtypeStruct((nb, ATT_V_COLS, L), bf16)
    return pl.pallas_call(
        _inproj_kernel,
        grid=(nb, L // tl),
        in_specs=[
            tok(D_MODEL),
            pl.BlockSpec((6, None, r, D_MODEL), lambda b, i: (0, b, 0, 0)),
            pl.BlockSpec((None, 1, D_MODEL), lambda b, i: (layer, 0, 0)),
            pl.BlockSpec((None, D_MODEL, IN_COLS_PAD), lambda b, i: (layer, 0, 0),
                         pipeline_mode=pl.Buffered(1)),
        ],
        out_specs=out_specs,
        out_shape=out_shape,
        compiler_params=_cparams(("parallel", "parallel")),
        name="in_proj",
    )(x, mod, g_pre, w_cat)


def _ssd_kernel(xbc_ref, z_ref, dt_ref, cw_ref, cb_ref, dtb_ref, alog_ref, dsk_ref, ng_ref, e_ref,
                et_ref, y_ref, st_ref, xpad_ref, *, T):
    c = pl.program_id(1)

    @pl.when(c == 0)
    def _():
        xpad_ref[0:SUBLANES, :] = jnp.zeros((SUBLANES, SSM_CONV_DIM), f32)
        st_ref[...] = jnp.zeros_like(st_ref)

    xr = xbc_ref[...]
    xpad_ref[SUBLANES:SUBLANES + T, :] = xr
    acc = xr * cw_ref[SSM_CONV - 1:SSM_CONV, :]
    for lag in range(1, SSM_CONV):
        acc = acc + (xpad_ref[SUBLANES - lag:SUBLANES - lag + T, :]
                     * cw_ref[SSM_CONV - 1 - lag:SSM_CONV - lag, :])
    xpad_ref[0:SUBLANES, :] = xpad_ref[T:T + SUBLANES, :]
    conv = _silu(acc + cb_ref[...])
    xs = conv[:, :SSM_D_INNER]
    bm = conv[:, SSM_D_INNER:SSM_D_INNER + SSM_BC].astype(bf16)
    cm = conv[:, SSM_D_INNER + SSM_BC:].astype(bf16)

    dt = _softplus(dt_ref[...] + dtb_ref[...])
    da = dt * (-jnp.exp(alog_ref[...]))
    row = lax.broadcasted_iota(jnp.int32, (T, T), 0)
    col = lax.broadcasted_iota(jnp.int32, (T, T), 1)
    causal = row >= col
    tri = causal.astype(bf16)
    acum = None
    for p in _split_bf16(da, 3):
        t = _dot(tri, p)
        acum = t if acum is None else acum + t
    acum_t = acum.T
    last = acum[T - 1:T, :]
    ea = jnp.exp(acum)
    to_end = jnp.exp(last - acum)

    e = e_ref[...]
    dt_f = _dot_split(dt, e, 2)
    ea_f = _dot_split(ea, e, 2)
    te_f = _dot_split(to_end, e, 2)
    xdt = xs * dt_f
    xdt_b = xdt.astype(bf16)
    xdtw_b = (xdt * te_f).astype(bf16)

    cd = jnp.exp(jnp.broadcast_to(acum_t[:, T - 1:T], (LANES, SSM_D_STATE)))
    cd_rows = None
    et = et_ref[...]
    for p in _split_bf16(cd, 2):
        t = _dot(et, p)
        cd_rows = t if cd_rows is None else cd_rows + t

    hpg = SSM_HEADS // SSM_GROUPS
    y_parts = []
    for g in range(SSM_GROUPS):
        bg = bm[:, g * SSM_D_STATE:(g + 1) * SSM_D_STATE]
        cg = cm[:, g * SSM_D_STATE:(g + 1) * SSM_D_STATE]
        cb = _dot_nt(cg, bg)
        rows = slice(g * SSM_GROUP_WIDTH, (g + 1) * SSM_GROUP_WIDTH)
        s_in = st_ref[rows, :]
        y_off = _dot_nt(cg, s_in.astype(bf16)) * ea_f[:, rows]
        diag = []
        for hh in range(hpg):
            h = g * hpg + hh
            seg = acum[:, h:h + 1] - acum_t[h:h + 1, :]
            dec = jnp.exp(jnp.where(causal, seg, -jnp.inf))
            m = (cb * dec).astype(bf16)
            diag.append(_dot(m, xdt_b[:, h * SSM_HEADDIM:(h + 1) * SSM_HEADDIM]))
        y_parts.append(jnp.concatenate(diag, axis=1) + y_off)
        st_new = _dot_tn(xdtw_b[:, rows], bg)
        st_ref[rows, :] = s_in * cd_rows[rows, :] + st_new
    y = jnp.concatenate(y_parts, axis=1) + dsk_ref[...] * xs
    yz = y * _silu(z_ref[...])
    outs = []
    for g in range(SSM_GROUPS):
        outs.append(_rms(yz[:, g * SSM_GROUP_WIDTH:(g + 1) * SSM_GROUP_WIDTH]))
    y_ref[...] = (jnp.concatenate(outs, axis=1) * ng_ref[...]).astype(bf16)


def _ssd_prompt(xbc, z, dt, cw, cb, dtb, alog, dsk, ng, e, et, layer, T):
    nb, L, _ = xbc.shape
    tok = lambda w: pl.BlockSpec((None, T, w), lambda b, i: (b, i, 0))
    par = lambda r, w: pl.BlockSpec((None, r, w), lambda b, i: (layer, 0, 0))
    return pl.pallas_call(
        functools.partial(_ssd_kernel, T=T),
        grid=(nb, L // T),
        in_specs=[tok(SSM_CONV_DIM), tok(SSM_D_INNER), tok(DT_PAD),
                  par(SSM_CONV, SSM_CONV_DIM), par(1, SSM_CONV_DIM), par(1, DT_PAD), par(1, DT_PAD),
                  par(1, SSM_D_INNER), par(1, SSM_D_INNER),
                  pl.BlockSpec((LANES, SSM_D_INNER), lambda b, i: (0, 0)),
                  pl.BlockSpec((SSM_D_INNER, LANES), lambda b, i: (0, 0))],
        out_specs=[tok(SSM_D_INNER),
                   pl.BlockSpec((None, SSM_D_INNER, SSM_D_STATE), lambda b, i: (b, 0, 0))],
        out_shape=[jax.ShapeDtypeStruct((nb, L, SSM_D_INNER), bf16),
                   jax.ShapeDtypeStruct((nb, SSM_D_INNER, SSM_D_STATE), f32)],
        scratch_shapes=[pltpu.VMEM((T + SUBLANES, SSM_CONV_DIM), f32)],
        compiler_params=_cparams(("parallel", "arbitrary")),
        name="ssd_prompt",
    )(xbc, z, dt, cw, cb, dtb, alog, dsk, ng, e, et)


def _lambda_value(lam_ref, lam_init):
    lv = lam_ref[...]
    a = jnp.sum(lv[0:1, :] * lv[1:2, :], axis=-1, keepdims=True)
    b = jnp.sum(lv[2:3, :] * lv[3:4, :], axis=-1, keepdims=True)
    return jnp.exp(a) - jnp.exp(b) + lam_init


def _sum_row_groups(x):
    acc = x[0:SUBLANES, :]
    for i in range(1, x.shape[0] // SUBLANES):
        acc = acc + x[i * SUBLANES:(i + 1) * SUBLANES, :]
    return acc


def _attn_kernel(qi_ref, ki_ref, q_ref, k_ref, v_ref, slope_ref, lam_ref, sgc_ref, o_ref,
                 m_ref, acc_ref, qa_ref, s_ref, p_ref, *, tq, tk, lam_init):
    g = pl.program_id(1)
    step = pl.program_id(2)
    qi = qi_ref[step]
    ki = ki_ref[step]
    half = ATT_QK_DIM

    @pl.when(ki == 0)
    def _():
        m_ref[...] = jnp.full_like(m_ref, -jnp.inf)
        acc_ref[...] = jnp.zeros_like(acc_ref)
        lane = lax.broadcasted_iota(jnp.int32, (tq, 2 * half), 1)
        q = q_ref[...]
        for r in range(ATT_GROUP):
            qv = q[:, r * 2 * half:(r + 1) * 2 * half]
            a0 = jnp.zeros((tq, 2 * half), f32)
            a1 = jnp.zeros((tq, 2 * half), f32)
            for t in range(POS_TERMS):
                term = slope_ref[g, r * POS_TERMS + t]
                a0 = jnp.where((lane >> 1) == (half >> 1) + t, term, a0)
                a1 = jnp.where((lane >> 1) == t, term, a1)
            qa_ref[2 * r] = jnp.where(lane < half, qv, a0.astype(bf16))
            qa_ref[2 * r + 1] = jnp.where(lane >= half, qv, a1.astype(bf16))

    def run(masked):
        v1 = jnp.concatenate([v_ref[...], jnp.ones((ONES_ROWS, tk), bf16)], axis=0)
        ka = (k_ref[:, :2 * half], k_ref[:, 2 * half:])
        if masked:
            keep = (lax.broadcasted_iota(jnp.int32, (tk, tq), 0)
                    <= lax.broadcasted_iota(jnp.int32, (tk, tq), 1))
        alphas = []
        for idx in range(4):
            s = _dot_nt(ka[idx % 2], qa_ref[idx])
            if masked:
                s = jnp.where(keep, s, -jnp.inf)
            s_ref[idx] = s
            m_old = m_ref[idx]
            m_new = jnp.maximum(m_old, jnp.max(s, axis=0, keepdims=True))
            alphas.append(jnp.exp2(m_old - m_new))
            m_ref[idx] = m_new
        grp = 2 * SUBLANES
        for idx in range(4):
            mg = jnp.broadcast_to(m_ref[idx], (grp, tq))
            for i in range(tk // grp):
                rows = slice(i * grp, (i + 1) * grp)
                p_ref[idx, rows, :] = jnp.exp2(s_ref[idx, rows, :] - mg).astype(bf16)
        for idx in range(4):
            acc_ref[idx] = alphas[idx] * acc_ref[idx] + _dot(v1, p_ref[idx])

    @pl.when(ki < qi)
    def _():
        run(False)

    @pl.when(ki == qi)
    def _():
        run(True)
        lam = _lambda_value(lam_ref, lam_init)
        dv = ATT_V_DIM
        for r in range(ATT_GROUP):
            o = (acc_ref[2 * r, :dv, :] / acc_ref[2 * r, dv:dv + 1, :]
                 - lam * (acc_ref[2 * r + 1, :dv, :] / acc_ref[2 * r + 1, dv:dv + 1, :]))
            o = o * lax.rsqrt(jnp.mean(o * o, axis=0, keepdims=True) + RMS_EPS)
            o = o * sgc_ref[...] * (1.0 - lam_init)
            o_ref[:, r * ATT_V_DIM:(r + 1) * ATT_V_DIM] = o.T.astype(bf16)


def _attn_prompt(q, kb, vb, slopes, lam_vecs, subln_col, layer, lam_init, tq):
    nb, L, _ = q.shape
    assert L <= 65536, "key positions are split into two 8-bit parts"
    nq = L // tq
    pairs = [(i, j) for i in range(nq) for j in range(i + 1)]
    qi_tab = jnp.asarray(np.array([a for a, _ in pairs], np.int32))
    ki_tab = jnp.asarray(np.array([b for _, b in pairs], np.int32))
    qw = ATT_GROUP * 2 * ATT_QK_DIM
    kw = 2 * LANES
    ow = ATT_GROUP * ATT_V_DIM
    grid_spec = pltpu.PrefetchScalarGridSpec(
        num_scalar_prefetch=2,
        grid=(nb, ATT_KV_HEADS, len(pairs)),
        in_specs=[
            pl.BlockSpec((None, tq, qw), lambda b, g, p, qi, ki: (b, qi[p], g)),
            pl.BlockSpec((None, tq, kw), lambda b, g, p, qi, ki: (b, ki[p], g)),
            pl.BlockSpec((None, ATT_V_DIM, tq), lambda b, g, p, qi, ki: (b, g, ki[p])),
            pl.BlockSpec(memory_space=pltpu.SMEM),
            pl.BlockSpec((None, 4, ATT_QK_DIM), lambda b, g, p, qi, ki: (layer, 0, 0)),
            pl.BlockSpec((None, ATT_V_DIM, 1), lambda b, g, p, qi, ki: (layer, 0, 0)),
        ],
        out_specs=pl.BlockSpec((None, tq, ow), lambda b, g, p, qi, ki: (b, qi[p], g)),
        scratch_shapes=[pltpu.VMEM((4, 1, tq), f32),
                        pltpu.VMEM((4, ATT_V_DIM + ONES_ROWS, tq), f32),
                        pltpu.VMEM((4, tq, 2 * ATT_QK_DIM), bf16),
                        pltpu.VMEM((4, tq, tq), f32), pltpu.VMEM((4, tq, tq), bf16)],
    )
    return pl.pallas_call(
        functools.partial(_attn_kernel, tq=tq, tk=tq, lam_init=lam_init),
        grid_spec=grid_spec,
        out_shape=jax.ShapeDtypeStruct((nb, L, ATT_WIDTH), bf16),
        compiler_params=_cparams(("parallel", "parallel", "arbitrary")),
        name="attn_prompt",
    )(qi_tab, ki_tab, q, kb, vb, slopes, lam_vecs, subln_col)


def _merge_kernel(ys_ref, ya_ref, gt_ref, x_ref, mod_ref, g_ref, wb_ref, wo_ref, o_ref):
    bp0 = _dot(ys_ref[...], wb_ref[0])
    bp1 = _dot(ya_ref[...], wb_ref[1])
    gt = gt_ref[...]
    merged = (jax.nn.sigmoid(gt[:, :D_MODEL]) * bp0 + jax.nn.sigmoid(gt[:, D_MODEL:]) * bp1)
    m = _dot(merged.astype(bf16), wo_ref[...])
    o_ref[...] = x_ref[...] + mod_ref[2] * (_rms(m) * g_ref[...])


def _merge(y_ssm, y_att, gates, x, mod, g_post, wb, wo, layer, tl):
    nb, L, _ = x.shape
    r = mod.shape[2]
    tok = lambda w: pl.BlockSpec((None, tl, w), lambda b, i: (b, i, 0))
    return pl.pallas_call(
        _merge_kernel,
        grid=(nb, L // tl),
        in_specs=[
            tok(SSM_D_INNER), tok(ATT_WIDTH), tok(2 * D_MODEL), tok(D_MODEL),
            pl.BlockSpec((6, None, r, D_MODEL), lambda b, i: (0, b, 0, 0)),
            pl.BlockSpec((None, 1, D_MODEL), lambda b, i: (layer, 0, 0)),
            pl.BlockSpec((None, 2, ATT_WIDTH, D_MODEL), lambda b, i: (layer, 0, 0, 0),
                         pipeline_mode=pl.Buffered(1)),
            pl.BlockSpec((None, D_MODEL, D_MODEL), lambda b, i: (layer, 0, 0),
                         pipeline_mode=pl.Buffered(1)),
        ],
        out_specs=tok(D_MODEL),
        out_shape=jax.ShapeDtypeStruct((nb, L, D_MODEL), f32),
        compiler_params=_cparams(("parallel", "parallel")),
        name="merge_out",
    )(y_ssm, y_att, gates, x, mod, g_post, wb, wo)


def _ffn_kernel(x_ref, mod_ref, g1_ref, g2_ref, wgu_ref, wd_ref, o_ref):
    x = x_ref[...]
    h = _rms(x) * g1_ref[...]
    hb = (h * (1.0 + mod_ref[4]) + mod_ref[3]).astype(bf16)
    gg = _dot(hb, wgu_ref[:, :FFN_HIDDEN])
    uu = _dot(hb, wgu_ref[:, FFN_HIDDEN:])
    f = _dot((_silu(gg) * uu).astype(bf16), wd_ref[...])
    o_ref[...] = x + mod_ref[5] * (_rms(f) * g2_ref[...])


def _ffn(x, mod, g_pre, g_post, wgu, wd, layer, tl):
    nb, L, _ = x.shape
    r = mod.shape[2]
    tok = pl.BlockSpec((None, tl, D_MODEL), lambda b, i: (b, i, 0))
    par = pl.BlockSpec((None, 1, D_MODEL), lambda b, i: (layer, 0, 0))
    return pl.pallas_call(
        _ffn_kernel,
        grid=(nb, L // tl),
        in_specs=[
            tok,
            pl.BlockSpec((6, None, r, D_MODEL), lambda b, i: (0, b, 0, 0)),
            par, par,
            pl.BlockSpec((None, D_MODEL, 2 * FFN_HIDDEN), lambda b, i: (layer, 0, 0),
                         pipeline_mode=pl.Buffered(1)),
            pl.BlockSpec((None, FFN_HIDDEN, D_MODEL), lambda b, i: (layer, 0, 0),
                         pipeline_mode=pl.Buffered(1)),
        ],
        out_specs=tok,
        out_shape=jax.ShapeDtypeStruct((nb, L, D_MODEL), f32),
        compiler_params=_cparams(("parallel", "parallel")),
        name="ffn",
    )(x, mod, g_pre, g_post, wgu, wd)


def _sample_pre_kernel(xbc_ref, sc_ref, dt_ref, cw_ref, cb_ref, dtb_ref, alog_ref, e_ref,
                       xs_ref, bc_ref, dec_ref, xdt_ref, cn_ref):
    xr = xbc_ref[...]
    acc = xr * cw_ref[SSM_CONV - 1:SSM_CONV, :]
    for j in range(SSM_CONV - 1):
        acc = acc + sc_ref[j] * cw_ref[j:j + 1, :]
    conv = _silu(acc + cb_ref[...])
    xs = conv[:, :SSM_D_INNER]
    xs_ref[...] = xs
    bc_ref[...] = conv[:, SSM_D_INNER:]
    for j in range(SSM_CONV - 2):
        cn_ref[j] = sc_ref[j + 1]
    cn_ref[SSM_CONV - 2] = xr
    dt = _softplus(dt_ref[...] + dtb_ref[...])
    da = dt * (-jnp.exp(alog_ref[...]))
    e = e_ref[...]
    dec = jnp.exp(_dot_split(da, e, 3))
    xdt = xs * _dot_split(dt, e, 3)
    for j, p in enumerate(_split_bf16(dec, 3)):
        dec_ref[j] = p
    for j, p in enumerate(_split_bf16(xdt, 2)):
        xdt_ref[j] = p


def _sample_pre(xbc, sc_t, dt, cw, cb, dtb, alog, e, layer):
    n = xbc.shape[0]
    full = lambda *s: pl.BlockSpec(s, lambda i: (0,) * len(s))
    par = lambda r, w: pl.BlockSpec((None, r, w), lambda i: (layer, 0, 0))
    return pl.pallas_call(
        _sample_pre_kernel,
        grid=(1,),
        in_specs=[full(n, SSM_CONV_DIM), full(SSM_CONV - 1, n, SSM_CONV_DIM), full(n, DT_PAD),
                  par(SSM_CONV, SSM_CONV_DIM), par(1, SSM_CONV_DIM), par(1, DT_PAD), par(1, DT_PAD),
                  full(LANES, SSM_D_INNER)],
        out_specs=[full(n, SSM_D_INNER), full(n, 2 * SSM_BC), full(3, n, SSM_D_INNER),
                   full(2, n, SSM_D_INNER), full(SSM_CONV - 1, n, SSM_CONV_DIM)],
        out_shape=[jax.ShapeDtypeStruct((n, SSM_D_INNER), f32),
                   jax.ShapeDtypeStruct((n, 2 * SSM_BC), f32),
                   jax.ShapeDtypeStruct((3, n, SSM_D_INNER), bf16),
                   jax.ShapeDtypeStruct((2, n, SSM_D_INNER), bf16),
                   jax.ShapeDtypeStruct((SSM_CONV - 1, n, SSM_CONV_DIM), f32)],
        compiler_params=_cparams(("arbitrary",)),
        name="sample_pre",
    )(xbc, sc_t, dt, cw, cb, dtb, alog, e)


def _sample_state_kernel(s_ref, dec_ref, xdt_ref, bc_ref, so_ref, y_ref, *, n):
    b = pl.program_id(0)

    def onehot(copies):
        rr = lax.broadcasted_iota(jnp.int32, (copies * n, SSM_D_STATE), 0)
        hit = rr == b
        for c in range(1, copies):
            hit = jnp.logical_or(hit, rr == b + c * n)
        return hit.astype(bf16)

    dm = _dot(dec_ref[...], onehot(3))
    xm = _dot(xdt_ref[...], onehot(2))
    brow = bc_ref[pl.ds(b, 1), :]
    y_parts = []
    for g in range(SSM_GROUPS):
        rows = slice(g * SSM_GROUP_WIDTH, (g + 1) * SSM_GROUP_WIDTH)
        bg = brow[:, g * SSM_D_STATE:(g + 1) * SSM_D_STATE]
        cg = brow[:, SSM_BC + g * SSM_D_STATE:SSM_BC + (g + 1) * SSM_D_STATE]
        s_new = s_ref[rows, :] * dm[rows, :] + xm[rows, :] * bg
        so_ref[rows, :] = s_new
        c8 = jnp.broadcast_to(cg, (SUBLANES, SSM_D_STATE)).astype(bf16)
        y_parts.append(_dot_nt(c8, s_new.astype(bf16))[0:1, :])
    y_ref[...] = jnp.concatenate(y_parts, axis=1)


def _sample_state(state, dec_t, xdt_t, bc, layer):
    n = bc.shape[0]
    return pl.pallas_call(
        functools.partial(_sample_state_kernel, n=n),
        grid=(n,),
        in_specs=[
            pl.BlockSpec((None, None, SSM_D_INNER, SSM_D_STATE), lambda b: (layer, b, 0, 0)),
            pl.BlockSpec((SSM_D_INNER, 3 * n), lambda b: (0, 0)),
            pl.BlockSpec((SSM_D_INNER, 2 * n), lambda b: (0, 0)),
            pl.BlockSpec((n, 2 * SSM_BC), lambda b: (0, 0)),
        ],
        out_specs=[pl.BlockSpec((None, SSM_D_INNER, SSM_D_STATE), lambda b: (b, 0, 0)),
                   pl.BlockSpec((None, 1, SSM_D_INNER), lambda b: (b, 0, 0))],
        out_shape=[jax.ShapeDtypeStruct((n, SSM_D_INNER, SSM_D_STATE), f32),
                   jax.ShapeDtypeStruct((n, 1, SSM_D_INNER), f32)],
        compiler_params=_cparams(("parallel",)),
        name="sample_state",
    )(state, dec_t, xdt_t, bc)


def _sample_gate_kernel(y_ref, xs_ref, z_ref, dsk_ref, ng_ref, o_ref):
    y = y_ref[...] + dsk_ref[...] * xs_ref[...]
    yz = y * _silu(z_ref[...])
    outs = []
    for g in range(SSM_GROUPS):
        outs.append(_rms(yz[:, g * SSM_GROUP_WIDTH:(g + 1) * SSM_GROUP_WIDTH]))
    o_ref[...] = (jnp.concatenate(outs, axis=1) * ng_ref[...]).astype(bf16)


def _sample_gate(y, xs, z, dsk, ng, layer):
    n = y.shape[0]
    full = pl.BlockSpec((n, SSM_D_INNER), lambda i: (0, 0))
    par = pl.BlockSpec((None, 1, SSM_D_INNER), lambda i: (layer, 0, 0))
    return pl.pallas_call(
        _sample_gate_kernel,
        grid=(1,),
        in_specs=[full, full, full, par, par],
        out_specs=full,
        out_shape=jax.ShapeDtypeStruct((n, SSM_D_INNER), bf16),
        compiler_params=_cparams(("arbitrary",)),
        name="sample_gate",
    )(y, xs, z, dsk, ng)


def _paged_attn_kernel(pt_ref, qm_ref, kn_ref, vn_ref, ns_ref, lam_ref, sg_ref, *rest, pps,
                       n_steps, past, lam_init):
    del pt_ref
    k_refs = rest[:pps]
    v_refs = rest[pps:2 * pps]
    o_ref = rest[2 * pps]
    m_ref, l_ref, acc_ref, s_scr = rest[2 * pps + 1:]
    j = pl.program_id(1)
    qm = qm_ref[...]
    nslope = ns_ref[...]

    @pl.when(j == 0)
    def _():
        s0 = _dot(kn_ref[...], qm)
        first = lax.broadcasted_iota(jnp.int32, s0.shape, 0) == 0
        s0 = jnp.where(first, s0, -jnp.inf)
        m0 = jnp.max(s0, axis=0, keepdims=True)
        p0 = jnp.exp2(s0 - m0)
        m_ref[...] = m0
        l_ref[...] = jnp.sum(p0, axis=0, keepdims=True)
        acc_ref[...] = _dot_tn(vn_ref[...], p0.astype(bf16))

    base = j * (pps * PAGE_SIZE)
    key = lax.broadcasted_iota(jnp.int32, (PAGE_SIZE, LANES), 0)
    for i in range(pps):
        s = _dot_tn(k_refs[i][...].astype(bf16), qm)
        dist = (past - (base + i * PAGE_SIZE) - key).astype(f32)
        s_scr[i * PAGE_SIZE:(i + 1) * PAGE_SIZE, :] = s + dist * nslope
    s_all = s_scr[...]
    m_old = m_ref[...]
    m_new = jnp.maximum(m_old, jnp.max(s_all, axis=0, keepdims=True))
    alpha = jnp.exp2(m_old - m_new)
    pe = jnp.exp2(s_all - m_new)
    l_ref[...] = alpha * l_ref[...] + jnp.sum(pe, axis=0, keepdims=True)
    pb = pe.astype(bf16)
    for g in range(ATT_KV_HEADS):
        pv = None
        for i in range(pps):
            vg = v_refs[i][pl.ds(g, PAGE_SIZE, stride=ATT_KV_HEADS), :].astype(bf16)
            t = _dot_tn(vg, pb[i * PAGE_SIZE:(i + 1) * PAGE_SIZE, :])
            pv = t if pv is None else pv + t
        rows = slice(g * ATT_V_DIM, (g + 1) * ATT_V_DIM)
        acc_ref[rows, :] = acc_ref[rows, :] * alpha + pv
    m_ref[...] = m_new

    @pl.when(j == n_steps - 1)
    def _():
        lam = _lambda_value(lam_ref, lam_init)
        inv_l = 1.0 / l_ref[...]
        for g in range(ATT_KV_HEADS):
            ot = (acc_ref[g * ATT_V_DIM:(g + 1) * ATT_V_DIM, :] * inv_l).T
            for r in range(ATT_GROUP):
                c0 = g * (ATT_GROUP * 2) + r * 2
                o = ot[c0:c0 + 1, :] - lam * ot[c0 + 1:c0 + 2, :]
                o = _rms(o) * sg_ref[...] * (1.0 - lam_init)
                h = g * ATT_GROUP + r
                o_ref[:, h * ATT_V_DIM:(h + 1) * ATT_V_DIM] = o.astype(bf16)


def _paged_attn(page_table, qmat, k_new, v_new, nslope, lam_vecs, subln_g, cache_kt, cache_v,
                layer, lam_init, pps):
    n, n_pages = page_table.shape
    n_steps = n_pages // pps
    past = n_pages * PAGE_SIZE

    def k_spec(i):
        return pl.BlockSpec((None, None, ATT_K_COLS, PAGE_SIZE),
                            lambda b, j, pt: (layer, pt[b, j * pps + i], 0, 0))

    def v_spec(i):
        return pl.BlockSpec((None, None, PAGE_SIZE * ATT_KV_HEADS, ATT_V_DIM),
                            lambda b, j, pt: (layer, pt[b, j * pps + i], 0, 0))

    grid_spec = pltpu.PrefetchScalarGridSpec(
        num_scalar_prefetch=1,
        grid=(n, n_steps),
        in_specs=[
            pl.BlockSpec((None, ATT_K_COLS, LANES), lambda b, j, pt: (b, 0, 0)),
            pl.BlockSpec((None, SUBLANES, ATT_K_COLS), lambda b, j, pt: (b, 0, 0)),
            pl.BlockSpec((None, SUBLANES, ATT_V_COLS), lambda b, j, pt: (b, 0, 0)),
            pl.BlockSpec((1, LANES), lambda b, j, pt: (0, 0)),
            pl.BlockSpec((None, 4, ATT_QK_DIM), lambda b, j, pt: (layer, 0, 0)),
            pl.BlockSpec((None, 1, ATT_V_DIM), lambda b, j, pt: (layer, 0, 0)),
        ] + [k_spec(i) for i in range(pps)] + [v_spec(i) for i in range(pps)],
        out_specs=pl.BlockSpec((None, 1, ATT_WIDTH), lambda b, j, pt: (b, 0, 0)),
        scratch_shapes=[pltpu.VMEM((1, LANES), f32), pltpu.VMEM((1, LANES), f32),
                        pltpu.VMEM((ATT_V_COLS, LANES), f32),
                        pltpu.VMEM((pps * PAGE_SIZE, LANES), f32)],
    )
    return pl.pallas_call(
        functools.partial(_paged_attn_kernel, pps=pps, n_steps=n_steps, past=past,
                          lam_init=lam_init),
        grid_spec=grid_spec,
        out_shape=jax.ShapeDtypeStruct((n, 1, ATT_WIDTH), bf16),
        compiler_params=_cparams(("parallel", "arbitrary")),
        name="paged_attn",
    )(page_table, qmat, k_new, v_new, nslope, lam_vecs, subln_g,
      *([cache_kt] * pps), *([cache_v] * pps))


def _alibi_tables():
    import ml_dtypes
    h = np.arange(1, ATT_HEADS + 1, dtype=np.float32)
    slopes = (2.0 ** (-8.0 * h / ATT_HEADS)).astype(np.float32) * np.float32(LOG2E)
    terms = np.zeros((ATT_HEADS, POS_TERMS), np.float32)
    rem = slopes.copy()
    for t in range(POS_TERMS):
        terms[:, t] = rem.astype(ml_dtypes.bfloat16).astype(np.float32)
        rem = rem - terms[:, t]
    per_col = np.zeros((1, LANES), np.float32)
    per_col[0, :ATT_SCORE_COLS] = -np.repeat(slopes, 2)
    return terms.reshape(ATT_KV_HEADS, ATT_GROUP * POS_TERMS), per_col


def _expand_matrix():
    e = np.zeros((LANES, SSM_D_INNER), np.float32)
    for h in range(SSM_HEADS):
        e[h, h * SSM_HEADDIM:(h + 1) * SSM_HEADDIM] = 1.0
    return e


def _pick(n, prefer):
    t = min(prefer, n)
    while n % t:
        t //= 2
    return t


def _query_matrix(q):
    n = q.shape[0]
    q5 = q.reshape(n, ATT_KV_HEADS, ATT_GROUP, 2, ATT_QK_DIM)
    eg = jnp.eye(ATT_KV_HEADS, dtype=q.dtype)
    ec = jnp.eye(2, dtype=q.dtype)
    qm = jnp.einsum('bgrcd,gh,ce->bgcdhre', q5, eg, ec)
    qm = qm.reshape(n, ATT_K_COLS, ATT_SCORE_COLS)
    return jnp.pad(qm, ((0, 0), (0, 0), (0, LANES - ATT_SCORE_COLS)))


def kernel(x_prompt, x_sample, cache_k, cache_v, state_conv, state_ssm, page_table, c_prompt,
           c_sample, w_ada, b_ada, g_pre_mix, g_post_mix, g_pre_ffn, g_post_ffn, w_in, conv_w,
           conv_b, dt_bias, a_log, d_skip, ssm_norm_g, lam_vecs, subln_g, w_branch, w_out, w_gu,
           w_down):
    depth = w_in.shape[0]
    bp, L, _ = x_prompt.shape
    n = x_sample.shape[0]
    n_pool = cache_k.shape[1]
    n_pages = page_table.shape[1]

    z_end = SSM_D_INNER
    xbc_end = z_end + SSM_CONV_DIM
    dt_end = xbc_end + SSM_HEADS
    q_end = dt_end + ATT_Q_COLS
    k_end = q_end + ATT_K_COLS
    v_end = k_end + ATT_V_COLS
    w_dt = jnp.pad(w_in[:, :, xbc_end:dt_end], ((0, 0), (0, 0), (0, DT_PAD - SSM_HEADS)))
    w_q = w_in[:, :, dt_end:q_end] * (ATT_QK_DIM ** -0.5 * LOG2E)
    w_k = w_in[:, :, q_end:k_end].reshape(depth, D_MODEL, ATT_KV_HEADS, 2, ATT_QK_DIM)
    zk = jnp.zeros_like(w_k[:, :, :, 0])
    w_kw = jnp.stack([w_k[:, :, :, 0], zk, zk, w_k[:, :, :, 1]], axis=3).reshape(
        depth, D_MODEL, ATT_KW_COLS)
    w_cat = jnp.concatenate(
        [w_in[:, :, :xbc_end], w_q, w_in[:, :, q_end:], w_dt, w_kw], axis=-1).astype(bf16)
    w_ada_b = w_ada.astype(bf16)
    wb_b = w_branch.astype(bf16)
    wo_b = w_out.astype(bf16)
    wgu_b = w_gu.astype(bf16)
    wd_b = w_down.astype(bf16)
    row = lambda a: a.reshape(depth, 1, a.shape[-1])
    pad_heads = lambda a: jnp.pad(a, ((0, 0), (0, DT_PAD - SSM_HEADS))).reshape(depth, 1, DT_PAD)
    dtb_p = pad_heads(dt_bias)
    alog_p = pad_heads(a_log)
    dsk_f = jnp.repeat(d_skip, SSM_HEADDIM, axis=-1).reshape(depth, 1, SSM_D_INNER)
    g_pre_mix, g_post_mix, g_pre_ffn, g_post_ffn = map(row, (g_pre_mix, g_post_mix, g_pre_ffn,
                                                            g_post_ffn))
    cb_r, ng_r, sg_r = row(conv_b), row(ssm_norm_g), row(subln_g)
    slopes_np, nslope_np = _alibi_tables()
    slopes = jnp.asarray(slopes_np)
    nslope = jnp.asarray(nslope_np)
    e_np = _expand_matrix()
    e_mat = jnp.asarray(e_np, dtype=bf16)
    et_mat = jnp.asarray(e_np.T, dtype=bf16)

    sg_col = subln_g.reshape(depth, ATT_V_DIM, 1)
    cache_kt = jnp.transpose(cache_k, (0, 1, 3, 4, 5, 2)).reshape(depth, n_pool, ATT_K_COLS, PAGE_SIZE)
    cache_vr = cache_v.reshape(depth, n_pool, PAGE_SIZE * ATT_KV_HEADS, ATT_V_DIM)
    state4 = state_ssm.reshape(depth, n, SSM_D_INNER, SSM_D_STATE)

    mod_all = _ada_mod(jnp.concatenate([c_prompt, c_sample], axis=0), w_ada_b, b_ada)

    tl_p = _pick(L, 256)
    t_ssd = _pick(L, 128)
    tq = _pick(L, 512)
    pps = _pick(n_pages, 8)

    xp = x_prompt
    xs = x_sample.reshape(1, n, D_MODEL)
    outs = [[] for _ in range(8)]
    for l in range(depth):
        lam_init = 0.8 - 0.6 * math.exp(-0.3 * l)
        mod_p = mod_all[l, :, :bp].reshape(6, bp, 1, D_MODEL)
        mod_s = mod_all[l, :, bp:].reshape(6, 1, n, D_MODEL)

        z, xbc, q, k, v, kb, vb, gates, dt = _in_proj(xp, mod_p, g_pre_mix, w_cat, l, tl_p)
        y_ssm, st = _ssd_prompt(xbc, z, dt, conv_w, cb_r, dtb_p, alog_p, dsk_f, ng_r, e_mat,
                                et_mat, l, t_ssd)
        y_att = _attn_prompt(q, kb, vb, slopes, lam_vecs, sg_col, l, lam_init, tq)
        x1 = _merge(y_ssm, y_att, gates, xp, mod_p, g_post_mix, wb_b, wo_b, l, tl_p)
        xp = _ffn(x1, mod_p, g_pre_ffn, g_post_ffn, wgu_b, wd_b, l, tl_p)
        outs[0].append(k.reshape(bp, L, ATT_KV_HEADS, 2, ATT_QK_DIM))
        outs[1].append(v.reshape(bp, L, ATT_KV_HEADS, ATT_V_DIM))
        outs[2].append(xbc[:, L - (SSM_CONV - 1):, :])
        outs[3].append(st.reshape(bp, SSM_HEADS, SSM_HEADDIM, SSM_D_STATE))

        z, xbc, q, k, v, kb, vb, gates, dt = _in_proj(xs, mod_s, g_pre_mix, w_cat, l, n)
        sc_t = jnp.transpose(state_conv[l], (1, 0, 2))
        xs_c, bc, dec3, xdt2, conv_new = _sample_pre(xbc[0], sc_t, dt[0], conv_w, cb_r, dtb_p,
                                                     alog_p, e_mat, l)
        dec_t = jnp.transpose(dec3, (2, 0, 1)).reshape(SSM_D_INNER, 3 * n)
        xdt_t = jnp.transpose(xdt2, (2, 0, 1)).reshape(SSM_D_INNER, 2 * n)
        st_s, y_raw = _sample_state(state4, dec_t, xdt_t, bc, l)
        y_ssm = _sample_gate(y_raw.reshape(n, SSM_D_INNER), xs_c, z[0], dsk_f, ng_r, l)
        qmat = _query_matrix(q[0])
        first_row = lambda a: jnp.pad(a[0].astype(bf16)[:, None, :],
                                      ((0, 0), (0, SUBLANES - 1), (0, 0)))
        y_att = _paged_attn(page_table, qmat, first_row(k), first_row(v), nslope, lam_vecs, sg_r,
                            cache_kt, cache_vr, l, lam_init, pps)
        x1 = _merge(y_ssm.reshape(1, n, SSM_D_INNER), y_att.reshape(1, n, ATT_WIDTH), gates, xs,
                    mod_s, g_post_mix, wb_b, wo_b, l, n)
        xs = _ffn(x1, mod_s, g_pre_ffn, g_post_ffn, wgu_b, wd_b, l, n)
        outs[4].append(k.reshape(n, 1, ATT_KV_HEADS, 2, ATT_QK_DIM))
        outs[5].append(v.reshape(n, 1, ATT_KV_HEADS, ATT_V_DIM))
        outs[6].append(jnp.transpose(conv_new, (1, 0, 2)))
        outs[7].append(st_s.reshape(n, SSM_HEADS, SSM_HEADDIM, SSM_D_STATE))

    stacked = [jnp.stack(o) for o in outs]
    return (xp, xs.reshape(n, 1, D_MODEL), *stacked)
```

```python
import functools
import math

import numpy as np
import jax
import jax.numpy as jnp
from jax import lax
from jax.experimental import pallas as pl
from jax.experimental.pallas import tpu as pltpu

D_MODEL = 1024
SSM_HEADDIM = 64
SSM_HEADS = 16
SSM_GROUPS = 2
SSM_D_STATE = 128
SSM_CONV = 4
SSM_D_INNER = SSM_HEADS * SSM_HEADDIM
SSM_BC = SSM_GROUPS * SSM_D_STATE
SSM_CONV_DIM = SSM_D_INNER + 2 * SSM_BC
SSM_GROUP_WIDTH = SSM_D_INNER // SSM_GROUPS
ATT_HEADS = 8
ATT_KV_HEADS = 4
ATT_GROUP = ATT_HEADS // ATT_KV_HEADS
ATT_QK_DIM = 64
ATT_V_DIM = 128
ATT_WIDTH = ATT_HEADS * ATT_V_DIM
ATT_Q_COLS = ATT_HEADS * 2 * ATT_QK_DIM
ATT_K_COLS = ATT_KV_HEADS * 2 * ATT_QK_DIM
ATT_V_COLS = ATT_KV_HEADS * ATT_V_DIM
ATT_SCORE_COLS = ATT_KV_HEADS * ATT_GROUP * 2
FFN_HIDDEN = 2816
RMS_EPS = 1e-6
PAGE_SIZE = 128
LANES = 128
SUBLANES = 8
DT_PAD = LANES

_SEG_Z = (0, SSM_D_INNER)
_SEG_XBC = (_SEG_Z[1], _SEG_Z[1] + SSM_CONV_DIM)
_SEG_Q = (_SEG_XBC[1], _SEG_XBC[1] + ATT_Q_COLS)
_SEG_K = (_SEG_Q[1], _SEG_Q[1] + ATT_K_COLS)
_SEG_V = (_SEG_K[1], _SEG_K[1] + ATT_V_COLS)
_SEG_G = (_SEG_V[1], _SEG_V[1] + 2 * D_MODEL)
_SEG_DT = (_SEG_G[1], _SEG_G[1] + DT_PAD)
ATT_KW_COLS = ATT_KV_HEADS * 2 * LANES
IN_COLS_PAD = _SEG_DT[1]
ATT_HEADS_PER_STEP = 4
PAGES_PER_STEP = 16
ONES_ROWS = 2 * SUBLANES
POS_TERMS = 3
LOG2E = math.log2(math.e)

VMEM_LIMIT = 56 * 1024 * 1024

f32 = jnp.float32
bf16 = jnp.bfloat16


def _cparams(sem):
    return pltpu.CompilerParams(dimension_semantics=sem, vmem_limit_bytes=VMEM_LIMIT)


def _rms(x):
    return x * lax.rsqrt(jnp.mean(x * x, axis=-1, keepdims=True) + RMS_EPS)


def _silu(x):
    return x * jax.nn.sigmoid(x)


def _softplus(x):
    return jnp.maximum(x, 0.0) + jnp.log(1.0 + jnp.exp(-jnp.abs(x)))


def _split_bf16(x, terms):
    parts = []
    rem = x
    for _ in range(terms):
        p = rem.astype(bf16)
        parts.append(p)
        rem = rem - p.astype(f32)
    return parts


def _dot(a, b):
    return jnp.dot(a, b, preferred_element_type=f32)


def _dot_nt(a, b):
    return lax.dot_general(a, b, (((1,), (1,)), ((), ())), preferred_element_type=f32)


def _dot_tn(a, b):
    return lax.dot_general(a, b, (((0,), (0,)), ((), ())), preferred_element_type=f32)


def _dot_split(x, w, terms):
    acc = None
    for p in _split_bf16(x, terms):
        t = _dot(p, w)
        acc = t if acc is None else acc + t
    return acc


def _ada_kernel(c_ref, w_ref, b_ref, o_ref):
    s = _silu(c_ref[...])
    o_ref[...] = _dot(s.astype(bf16), w_ref[...]) + b_ref[...]


def _ada_mod(c_all, w_ada_b, b_ada):
    depth = w_ada_b.shape[0]
    rows = c_all.shape[0]
    return pl.pallas_call(
        _ada_kernel,
        grid=(depth, 6),
        in_specs=[
            pl.BlockSpec((rows, D_MODEL), lambda l, j: (0, 0)),
            pl.BlockSpec((None, D_MODEL, D_MODEL), lambda l, j: (l, 0, j)),
            pl.BlockSpec((None, 1, D_MODEL), lambda l, j: (l, 0, j)),
        ],
        out_specs=pl.BlockSpec((None, None, rows, D_MODEL), lambda l, j: (l, j, 0, 0)),
        out_shape=jax.ShapeDtypeStruct((depth, 6, rows, D_MODEL), f32),
        compiler_params=_cparams(("arbitrary", "arbitrary")),
        name="ada_mod",
    )(c_all, w_ada_b, b_ada.reshape(depth, 1, 6 * D_MODEL))


def _inproj_kernel(x_ref, mod_ref, g_ref, w_ref, *rest):
    z_ref, xbc_ref, q_ref, k_ref, v_ref, kb_ref, vb_ref, gt_ref, dt_ref = rest[-9:]
    h = _rms(x_ref[...]) * g_ref[...]
    h = h * (1.0 + mod_ref[1]) + mod_ref[0]
    hb = h.astype(bf16)

    def seg(lo_hi):
        return _dot(hb, w_ref[:, lo_hi[0]:lo_hi[1]])

    z_ref[...] = seg(_SEG_Z)
    xbc_ref[...] = seg(_SEG_XBC)
    q_ref[...] = seg(_SEG_Q).astype(bf16)
    kk = seg(_SEG_K)
    k_ref[...] = kk.T
    rows = x_ref.shape[0]
    lane = lax.broadcasted_iota(jnp.int32, (rows, LANES), 1)
    pos = pl.program_id(1) * rows + lax.broadcasted_iota(jnp.int32, (rows, LANES), 0)
    part = jnp.where((lane & 1) == 0, ((pos >> 8) << 8).astype(f32), (pos & 255).astype(f32))
    off = lane & (ATT_QK_DIM - 1)
    extra = jnp.where(off < 2 * POS_TERMS, part, 0.0)
    for g in range(ATT_KV_HEADS):
        kv = kk[:, g * LANES:(g + 1) * LANES]
        kb_ref[:, 2 * g * LANES:(2 * g + 1) * LANES] = jnp.where(
            lane < ATT_QK_DIM, kv, extra).astype(bf16)
        kb_ref[:, (2 * g + 1) * LANES:(2 * g + 2) * LANES] = jnp.where(
            lane >= ATT_QK_DIM, kv, extra).astype(bf16)
    vv = seg(_SEG_V)
    for g in range(ATT_KV_HEADS):
        v_ref[pl.ds(g, rows, stride=ATT_KV_HEADS), :] = vv[:, g * ATT_V_DIM:(g + 1) * ATT_V_DIM]
    vb_ref[...] = vv.T.astype(bf16)
    gt_ref[...] = seg(_SEG_G)
    dt_ref[...] = seg(_SEG_DT)


K_OUT, V_OUT, VT_OUT = 3, 4, 6


def _in_proj(x, mod, g_pre, w_cat, layer, depth, tl, kv_buffers=None):
    nb, L, _ = x.shape
    r = mod.shape[2]
    widths = [(SSM_D_INNER, f32), (SSM_CONV_DIM, f32), (ATT_Q_COLS, bf16), (ATT_K_COLS, f32),
              (ATT_V_COLS, f32), (ATT_KW_COLS, bf16), (ATT_V_COLS, bf16), (2 * D_MODEL, f32),
              (DT_PAD, f32)]
    tok = lambda w: pl.BlockSpec((None, tl, w), lambda b, i: (b, i, 0))
    out_specs = [tok(w) for w, _ in widths]
    out_shape = [jax.ShapeDtypeStruct((nb, L, w), dt) for w, dt in widths]
    out_specs[VT_OUT] = pl.BlockSpec((None, ATT_V_COLS, tl), lambda b, i: (b, 0, i))
    out_shape[VT_OUT] = jax.ShapeDtypeStruct((nb, ATT_V_COLS, L), bf16)
    out_specs[K_OUT] = pl.BlockSpec((None, None, ATT_K_COLS, tl), lambda b, i: (layer, b, 0, i))
    out_shape[K_OUT] = jax.ShapeDtypeStruct((depth, nb, ATT_K_COLS, L), f32)
    out_specs[V_OUT] = pl.BlockSpec((None, None, tl * ATT_KV_HEADS, ATT_V_DIM),
                                    lambda b, i: (layer, b, i, 0))
    out_shape[V_OUT] = jax.ShapeDtypeStruct((depth, nb, L * ATT_KV_HEADS, ATT_V_DIM), f32)
    in_specs = [
        tok(D_MODEL),
        pl.BlockSpec((6, None, r, D_MODEL), lambda b, i: (0, b, 0, 0)),
        pl.BlockSpec((None, 1, D_MODEL), lambda b, i: (layer, 0, 0)),
        pl.BlockSpec((None, D_MODEL, IN_COLS_PAD), lambda b, i: (layer, 0, 0),
                     pipeline_mode=pl.Buffered(1)),
    ]
    args = [x, mod, g_pre, w_cat]
    aliases = {}
    if kv_buffers is not None:
        aliases = {len(args): K_OUT, len(args) + 1: V_OUT}
        in_specs += [pl.BlockSpec(memory_space=pl.ANY), pl.BlockSpec(memory_space=pl.ANY)]
        args += list(kv_buffers)
    return pl.pallas_call(
        _inproj_kernel,
        grid=(nb, L // tl),
        in_specs=in_specs,
        out_specs=out_specs,
        out_shape=out_shape,
        input_output_aliases=aliases,
        compiler_params=_cparams(("parallel", "parallel")),
        name="in_proj",
    )(*args)


def _ssd_kernel(xbc_ref, z_ref, dt_ref, cw_ref, cb_ref, dtb_ref, alog_ref, dsk_ref, ng_ref, e_ref,
                et_ref, y_ref, st_ref, xpad_ref, *, T):
    c = pl.program_id(1)

    @pl.when(c == 0)
    def _():
        xpad_ref[0:SUBLANES, :] = jnp.zeros((SUBLANES, SSM_CONV_DIM), f32)
        st_ref[...] = jnp.zeros_like(st_ref)

    xr = xbc_ref[...]
    xpad_ref[SUBLANES:SUBLANES + T, :] = xr
    acc = xr * cw_ref[SSM_CONV - 1:SSM_CONV, :]
    for lag in range(1, SSM_CONV):
        acc = acc + (xpad_ref[SUBLANES - lag:SUBLANES - lag + T, :]
                     * cw_ref[SSM_CONV - 1 - lag:SSM_CONV - lag, :])
    xpad_ref[0:SUBLANES, :] = xpad_ref[T:T + SUBLANES, :]
    conv = _silu(acc + cb_ref[...])
    xs = conv[:, :SSM_D_INNER]
    bm = conv[:, SSM_D_INNER:SSM_D_INNER + SSM_BC].astype(bf16)
    cm = conv[:, SSM_D_INNER + SSM_BC:].astype(bf16)

    dt = _softplus(dt_ref[...] + dtb_ref[...])
    da = dt * (-jnp.exp(alog_ref[...]))
    row = lax.broadcasted_iota(jnp.int32, (T, T), 0)
    col = lax.broadcasted_iota(jnp.int32, (T, T), 1)
    causal = row >= col
    tri = causal.astype(bf16)
    acum = None
    for p in _split_bf16(da, 3):
        t = _dot(tri, p)
        acum = t if acum is None else acum + t
    acum_t = acum.T
    last = acum[T - 1:T, :]
    ea = jnp.exp(acum)
    to_end = jnp.exp(last - acum)

    e = e_ref[...]
    dt_f = _dot_split(dt, e, 2)
    ea_f = _dot_split(ea, e, 2)
    te_f = _dot_split(to_end, e, 2)
    xdt = xs * dt_f
    xdt_b = xdt.astype(bf16)
    xdtw_b = (xdt * te_f).astype(bf16)

    cd = jnp.exp(jnp.broadcast_to(acum_t[:, T - 1:T], (LANES, SSM_D_STATE)))
    cd_rows = None
    et = et_ref[...]
    for p in _split_bf16(cd, 2):
        t = _dot(et, p)
        cd_rows = t if cd_rows is None else cd_rows + t

    hpg = SSM_HEADS // SSM_GROUPS
    y_parts = []
    for g in range(SSM_GROUPS):
        bg = bm[:, g * SSM_D_STATE:(g + 1) * SSM_D_STATE]
        cg = cm[:, g * SSM_D_STATE:(g + 1) * SSM_D_STATE]
        cb = _dot_nt(cg, bg)
        rows = slice(g * SSM_GROUP_WIDTH, (g + 1) * SSM_GROUP_WIDTH)
        s_in = st_ref[rows, :]
        y_off = _dot_nt(cg, s_in.astype(bf16)) * ea_f[:, rows]
        diag = []
        for hh in range(hpg):
            h = g * hpg + hh
            seg = acum[:, h:h + 1] - acum_t[h:h + 1, :]
            dec = jnp.exp(jnp.where(causal, seg, -jnp.inf))
            m = (cb * dec).astype(bf16)
            diag.append(_dot(m, xdt_b[:, h * SSM_HEADDIM:(h + 1) * SSM_HEADDIM]))
        y_parts.append(jnp.concatenate(diag, axis=1) + y_off)
        st_new = _dot_tn(xdtw_b[:, rows], bg)
        st_ref[rows, :] = s_in * cd_rows[rows, :] + st_new
    y = jnp.concatenate(y_parts, axis=1) + dsk_ref[...] * xs
    yz = y * _silu(z_ref[...])
    outs = []
    for g in range(SSM_GROUPS):
        outs.append(_rms(yz[:, g * SSM_GROUP_WIDTH:(g + 1) * SSM_GROUP_WIDTH]))
    y_ref[...] = (jnp.concatenate(outs, axis=1) * ng_ref[...]).astype(bf16)


def _ssd_prompt(xbc, z, dt, cw, cb, dtb, alog, dsk, ng, e, et, layer, T):
    nb, L, _ = xbc.shape
    tok = lambda w: pl.BlockSpec((None, T, w), lambda b, i: (b, i, 0))
    par = lambda r, w: pl.BlockSpec((None, r, w), lambda b, i: (layer, 0, 0))
    return pl.pallas_call(
        functools.partial(_ssd_kernel, T=T),
        grid=(nb, L // T),
        in_specs=[tok(SSM_CONV_DIM), tok(SSM_D_INNER), tok(DT_PAD),
                  par(SSM_CONV, SSM_CONV_DIM), par(1, SSM_CONV_DIM), par(1, DT_PAD), par(1, DT_PAD),
                  par(1, SSM_D_INNER), par(1, SSM_D_INNER),
                  pl.BlockSpec((LANES, SSM_D_INNER), lambda b, i: (0, 0)),
                  pl.BlockSpec((SSM_D_INNER, LANES), lambda b, i: (0, 0))],
        out_specs=[tok(SSM_D_INNER),
                   pl.BlockSpec((None, SSM_D_INNER, SSM_D_STATE), lambda b, i: (b, 0, 0))],
        out_shape=[jax.ShapeDtypeStruct((nb, L, SSM_D_INNER), bf16),
                   jax.ShapeDtypeStruct((nb, SSM_D_INNER, SSM_D_STATE), f32)],
        scratch_shapes=[pltpu.VMEM((T + SUBLANES, SSM_CONV_DIM), f32)],
        compiler_params=_cparams(("parallel", "arbitrary")),
        name="ssd_prompt",
    )(xbc, z, dt, cw, cb, dtb, alog, dsk, ng, e, et)


def _lambda_value(lam_ref, lam_init):
    lv = lam_ref[...]
    a = jnp.sum(lv[0:1, :] * lv[1:2, :], axis=-1, keepdims=True)
    b = jnp.sum(lv[2:3, :] * lv[3:4, :], axis=-1, keepdims=True)
    return jnp.exp(a) - jnp.exp(b) + lam_init


def _sum_row_groups(x):
    acc = x[0:SUBLANES, :]
    for i in range(1, x.shape[0] // SUBLANES):
        acc = acc + x[i * SUBLANES:(i + 1) * SUBLANES, :]
    return acc


def _attn_kernel(qi_ref, ki_ref, q_ref, k_ref, v_ref, slope_ref, lam_ref, sgc_ref, o_ref,
                 m_ref, acc_ref, qa_ref, s_ref, p_ref, *, tq, tk, lam_init, gu):
    g0 = pl.program_id(1) * gu
    step = pl.program_id(2)
    qi = qi_ref[step]
    ki = ki_ref[step]
    half = ATT_QK_DIM
    qw = ATT_GROUP * 2 * half
    kw = 2 * LANES
    dv = ATT_V_DIM

    @pl.when(ki == 0)
    def _():
        m_ref[...] = jnp.full_like(m_ref, -jnp.inf)
        acc_ref[...] = jnp.zeros_like(acc_ref)
        lane = lax.broadcasted_iota(jnp.int32, (tq, 2 * half), 1)
        for gl in range(gu):
            for r in range(ATT_GROUP):
                qv = q_ref[:, gl * qw + r * 2 * half:gl * qw + (r + 1) * 2 * half]
                a0 = jnp.zeros((tq, 2 * half), f32)
                a1 = jnp.zeros((tq, 2 * half), f32)
                for t in range(POS_TERMS):
                    term = slope_ref[g0 + gl, r * POS_TERMS + t]
                    a0 = jnp.where((lane >> 1) == (half >> 1) + t, term, a0)
                    a1 = jnp.where((lane >> 1) == t, term, a1)
                qa_ref[4 * gl + 2 * r] = jnp.where(lane < half, qv, a0.astype(bf16))
                qa_ref[4 * gl + 2 * r + 1] = jnp.where(lane >= half, qv, a1.astype(bf16))

    def run(masked, gl):
        ones = jnp.ones((ONES_ROWS, tk), bf16)
        v1 = jnp.concatenate([v_ref[gl * dv:(gl + 1) * dv, :], ones], axis=0)
        ka = (k_ref[:, gl * kw:gl * kw + 2 * half], k_ref[:, gl * kw + 2 * half:(gl + 1) * kw])
        if masked:
            keep = (lax.broadcasted_iota(jnp.int32, (tk, tq), 0)
                    <= lax.broadcasted_iota(jnp.int32, (tk, tq), 1))
        alphas = []
        for c in range(4):
            idx = 4 * gl + c
            s = _dot_nt(ka[c % 2], qa_ref[idx])
            if masked:
                s = jnp.where(keep, s, -jnp.inf)
            s_ref[idx] = s
            m_old = m_ref[idx]
            m_new = jnp.maximum(m_old, jnp.max(s, axis=0, keepdims=True))
            alphas.append(jnp.exp2(m_old - m_new))
            m_ref[idx] = m_new
        grp = 2 * SUBLANES
        for c in range(4):
            idx = 4 * gl + c
            mg = jnp.broadcast_to(m_ref[idx], (grp, tq))
            for i in range(tk // grp):
                rows = slice(i * grp, (i + 1) * grp)
                p_ref[idx, rows, :] = jnp.exp2(s_ref[idx, rows, :] - mg).astype(bf16)
        for c in range(4):
            idx = 4 * gl + c
            acc_ref[idx] = alphas[c] * acc_ref[idx] + _dot(v1, p_ref[idx])

    def finish(gl):
        lam = _lambda_value(lam_ref, lam_init)
        for r in range(ATT_GROUP):
            i0 = 4 * gl + 2 * r
            o = (acc_ref[i0, :dv, :] / acc_ref[i0, dv:dv + 1, :]
                 - lam * (acc_ref[i0 + 1, :dv, :] / acc_ref[i0 + 1, dv:dv + 1, :]))
            o = o * lax.rsqrt(jnp.mean(o * o, axis=0, keepdims=True) + RMS_EPS)
            o = o * sgc_ref[...] * (1.0 - lam_init)
            col = (gl * ATT_GROUP + r) * dv
            o_ref[:, col:col + dv] = o.T.astype(bf16)

    @pl.when(ki < qi)
    def _():
        for gl in range(gu):
            run(False, gl)

    @pl.when(ki == qi)
    def _():
        for gl in range(gu):
            run(True, gl)
            finish(gl)


def _attn_prompt(q, kb, vt, slopes, lam_vecs, subln_col, layer, lam_init, tq, gu):
    nb, L, _ = q.shape
    assert L <= 65536, "key positions are split into two 8-bit parts"
    nq = L // tq
    pairs = [(i, j) for i in range(nq) for j in range(i + 1)]
    qi_tab = jnp.asarray(np.array([a for a, _ in pairs], np.int32))
    ki_tab = jnp.asarray(np.array([b for _, b in pairs], np.int32))
    qw = gu * ATT_GROUP * 2 * ATT_QK_DIM
    kw = gu * 2 * LANES
    ow = gu * ATT_GROUP * ATT_V_DIM
    grid_spec = pltpu.PrefetchScalarGridSpec(
        num_scalar_prefetch=2,
        grid=(nb, ATT_KV_HEADS // gu, len(pairs)),
        in_specs=[
            pl.BlockSpec((None, tq, qw), lambda b, g, p, qi, ki: (b, qi[p], g)),
            pl.BlockSpec((None, tq, kw), lambda b, g, p, qi, ki: (b, ki[p], g)),
            pl.BlockSpec((None, gu * ATT_V_DIM, tq), lambda b, g, p, qi, ki: (b, g, ki[p])),
            pl.BlockSpec(memory_space=pltpu.SMEM),
            pl.BlockSpec((None, 4, ATT_QK_DIM), lambda b, g, p, qi, ki: (layer, 0, 0)),
            pl.BlockSpec((None, ATT_V_DIM, 1), lambda b, g, p, qi, ki: (layer, 0, 0)),
        ],
        out_specs=pl.BlockSpec((None, tq, ow), lambda b, g, p, qi, ki: (b, qi[p], g)),
        scratch_shapes=[pltpu.VMEM((4 * gu, 1, tq), f32),
                        pltpu.VMEM((4 * gu, ATT_V_DIM + ONES_ROWS, tq), f32),
                        pltpu.VMEM((4 * gu, tq, 2 * ATT_QK_DIM), bf16),
                        pltpu.VMEM((4 * gu, tq, tq), f32), pltpu.VMEM((4 * gu, tq, tq), bf16)],
    )
    return pl.pallas_call(
        functools.partial(_attn_kernel, tq=tq, tk=tq, lam_init=lam_init, gu=gu),
        grid_spec=grid_spec,
        out_shape=jax.ShapeDtypeStruct((nb, L, ATT_WIDTH), bf16),
        compiler_params=_cparams(("parallel", "parallel", "arbitrary")),
        name="attn_prompt",
    )(qi_tab, ki_tab, q, kb, vt, slopes, lam_vecs, subln_col)


def _merge_kernel(ys_ref, ya_ref, gt_ref, x_ref, mod_ref, g_ref, wb_ref, wo_ref, o_ref):
    bp0 = _dot(ys_ref[...], wb_ref[0])
    bp1 = _dot(ya_ref[...], wb_ref[1])
    gt = gt_ref[...]
    merged = (jax.nn.sigmoid(gt[:, :D_MODEL]) * bp0 + jax.nn.sigmoid(gt[:, D_MODEL:]) * bp1)
    m = _dot(merged.astype(bf16), wo_ref[...])
    o_ref[...] = x_ref[...] + mod_ref[2] * (_rms(m) * g_ref[...])


def _merge(y_ssm, y_att, gates, x, mod, g_post, wb, wo, layer, tl):
    nb, L, _ = x.shape
    r = mod.shape[2]
    tok = lambda w: pl.BlockSpec((None, tl, w), lambda b, i: (b, i, 0))
    return pl.pallas_call(
        _merge_kernel,
        grid=(nb, L // tl),
        in_specs=[
            tok(SSM_D_INNER), tok(ATT_WIDTH), tok(2 * D_MODEL), tok(D_MODEL),
            pl.BlockSpec((6, None, r, D_MODEL), lambda b, i: (0, b, 0, 0)),
            pl.BlockSpec((None, 1, D_MODEL), lambda b, i: (layer, 0, 0)),
            pl.BlockSpec((None, 2, ATT_WIDTH, D_MODEL), lambda b, i: (layer, 0, 0, 0),
                         pipeline_mode=pl.Buffered(1)),
            pl.BlockSpec((None, D_MODEL, D_MODEL), lambda b, i: (layer, 0, 0),
                         pipeline_mode=pl.Buffered(1)),
        ],
        out_specs=tok(D_MODEL),
        out_shape=jax.ShapeDtypeStruct((nb, L, D_MODEL), f32),
        compiler_params=_cparams(("parallel", "parallel")),
        name="merge_out",
    )(y_ssm, y_att, gates, x, mod, g_post, wb, wo)


def _ffn_kernel(x_ref, mod_ref, g1_ref, g2_ref, wgu_ref, wd_ref, o_ref):
    x = x_ref[...]
    h = _rms(x) * g1_ref[...]
    hb = (h * (1.0 + mod_ref[4]) + mod_ref[3]).astype(bf16)
    gg = _dot(hb, wgu_ref[:, :FFN_HIDDEN])
    uu = _dot(hb, wgu_ref[:, FFN_HIDDEN:])
    f = _dot((_silu(gg) * uu).astype(bf16), wd_ref[...])
    o_ref[...] = x + mod_ref[5] * (_rms(f) * g2_ref[...])


def _ffn(x, mod, g_pre, g_post, wgu, wd, layer, tl):
    nb, L, _ = x.shape
    r = mod.shape[2]
    tok = pl.BlockSpec((None, tl, D_MODEL), lambda b, i: (b, i, 0))
    par = pl.BlockSpec((None, 1, D_MODEL), lambda b, i: (layer, 0, 0))
    return pl.pallas_call(
        _ffn_kernel,
        grid=(nb, L // tl),
        in_specs=[
            tok,
            pl.BlockSpec((6, None, r, D_MODEL), lambda b, i: (0, b, 0, 0)),
            par, par,
            pl.BlockSpec((None, D_MODEL, 2 * FFN_HIDDEN), lambda b, i: (layer, 0, 0),
                         pipeline_mode=pl.Buffered(1)),
            pl.BlockSpec((None, FFN_HIDDEN, D_MODEL), lambda b, i: (layer, 0, 0),
                         pipeline_mode=pl.Buffered(1)),
        ],
        out_specs=tok,
        out_shape=jax.ShapeDtypeStruct((nb, L, D_MODEL), f32),
        compiler_params=_cparams(("parallel", "parallel")),
        name="ffn",
    )(x, mod, g_pre, g_post, wgu, wd)


def _sample_pre_kernel(xbc_ref, sc_ref, dt_ref, cw_ref, cb_ref, dtb_ref, alog_ref, e_ref,
                       xs_ref, bc_ref, dec_ref, xdt_ref, cn_ref):
    xr = xbc_ref[...]
    acc = xr * cw_ref[SSM_CONV - 1:SSM_CONV, :]
    for j in range(SSM_CONV - 1):
        acc = acc + sc_ref[j] * cw_ref[j:j + 1, :]
    conv = _silu(acc + cb_ref[...])
    xs = conv[:, :SSM_D_INNER]
    xs_ref[...] = xs
    bc_ref[...] = conv[:, SSM_D_INNER:]
    for j in range(SSM_CONV - 2):
        cn_ref[j] = sc_ref[j + 1]
    cn_ref[SSM_CONV - 2] = xr
    dt = _softplus(dt_ref[...] + dtb_ref[...])
    da = dt * (-jnp.exp(alog_ref[...]))
    e = e_ref[...]
    dec = jnp.exp(_dot_split(da, e, 3))
    xdt = xs * _dot_split(dt, e, 3)
    for j, p in enumerate(_split_bf16(dec, 3)):
        dec_ref[j] = p
    for j, p in enumerate(_split_bf16(xdt, 2)):
        xdt_ref[j] = p


def _sample_pre(xbc, sc_t, dt, cw, cb, dtb, alog, e, layer):
    n = xbc.shape[0]
    full = lambda *s: pl.BlockSpec(s, lambda i: (0,) * len(s))
    par = lambda r, w: pl.BlockSpec((None, r, w), lambda i: (layer, 0, 0))
    return pl.pallas_call(
        _sample_pre_kernel,
        grid=(1,),
        in_specs=[full(n, SSM_CONV_DIM), full(SSM_CONV - 1, n, SSM_CONV_DIM), full(n, DT_PAD),
                  par(SSM_CONV, SSM_CONV_DIM), par(1, SSM_CONV_DIM), par(1, DT_PAD), par(1, DT_PAD),
                  full(LANES, SSM_D_INNER)],
        out_specs=[full(n, SSM_D_INNER), full(n, 2 * SSM_BC), full(3, n, SSM_D_INNER),
                   full(2, n, SSM_D_INNER), full(SSM_CONV - 1, n, SSM_CONV_DIM)],
        out_shape=[jax.ShapeDtypeStruct((n, SSM_D_INNER), f32),
                   jax.ShapeDtypeStruct((n, 2 * SSM_BC), f32),
                   jax.ShapeDtypeStruct((3, n, SSM_D_INNER), bf16),
                   jax.ShapeDtypeStruct((2, n, SSM_D_INNER), bf16),
                   jax.ShapeDtypeStruct((SSM_CONV - 1, n, SSM_CONV_DIM), f32)],
        compiler_params=_cparams(("arbitrary",)),
        name="sample_pre",
    )(xbc, sc_t, dt, cw, cb, dtb, alog, e)


def _sample_state_kernel(s_ref, dec_ref, xdt_ref, bc_ref, so_ref, y_ref, *, n):
    b = pl.program_id(0)

    def onehot(copies):
        rr = lax.broadcasted_iota(jnp.int32, (copies * n, SSM_D_STATE), 0)
        hit = rr == b
        for c in range(1, copies):
            hit = jnp.logical_or(hit, rr == b + c * n)
        return hit.astype(bf16)

    dm = _dot(dec_ref[...], onehot(3))
    xm = _dot(xdt_ref[...], onehot(2))
    brow = bc_ref[pl.ds(b, 1), :]
    y_parts = []
    for g in range(SSM_GROUPS):
        rows = slice(g * SSM_GROUP_WIDTH, (g + 1) * SSM_GROUP_WIDTH)
        bg = brow[:, g * SSM_D_STATE:(g + 1) * SSM_D_STATE]
        cg = brow[:, SSM_BC + g * SSM_D_STATE:SSM_BC + (g + 1) * SSM_D_STATE]
        s_new = s_ref[rows, :] * dm[rows, :] + xm[rows, :] * bg
        so_ref[rows, :] = s_new
        c8 = jnp.broadcast_to(cg, (SUBLANES, SSM_D_STATE)).astype(bf16)
        y_parts.append(_dot_nt(c8, s_new.astype(bf16))[0:1, :])
    y_ref[...] = jnp.concatenate(y_parts, axis=1)


def _sample_state(state, dec_t, xdt_t, bc, layer):
    n = bc.shape[0]
    return pl.pallas_call(
        functools.partial(_sample_state_kernel, n=n),
        grid=(n,),
        in_specs=[
            pl.BlockSpec((None, None, SSM_D_INNER, SSM_D_STATE), lambda b: (layer, b, 0, 0)),
            pl.BlockSpec((SSM_D_INNER, 3 * n), lambda b: (0, 0)),
            pl.BlockSpec((SSM_D_INNER, 2 * n), lambda b: (0, 0)),
            pl.BlockSpec((n, 2 * SSM_BC), lambda b: (0, 0)),
        ],
        out_specs=[pl.BlockSpec((None, SSM_D_INNER, SSM_D_STATE), lambda b: (b, 0, 0)),
                   pl.BlockSpec((None, 1, SSM_D_INNER), lambda b: (b, 0, 0))],
        out_shape=[jax.ShapeDtypeStruct((n, SSM_D_INNER, SSM_D_STATE), f32),
                   jax.ShapeDtypeStruct((n, 1, SSM_D_INNER), f32)],
        compiler_params=_cparams(("parallel",)),
        name="sample_state",
    )(state, dec_t, xdt_t, bc)


def _sample_gate_kernel(y_ref, xs_ref, z_ref, dsk_ref, ng_ref, o_ref):
    y = y_ref[...] + dsk_ref[...] * xs_ref[...]
    yz = y * _silu(z_ref[...])
    outs = []
    for g in range(SSM_GROUPS):
        outs.append(_rms(yz[:, g * SSM_GROUP_WIDTH:(g + 1) * SSM_GROUP_WIDTH]))
    o_ref[...] = (jnp.concatenate(outs, axis=1) * ng_ref[...]).astype(bf16)


def _sample_gate(y, xs, z, dsk, ng, layer):
    n = y.shape[0]
    full = pl.BlockSpec((n, SSM_D_INNER), lambda i: (0, 0))
    par = pl.BlockSpec((None, 1, SSM_D_INNER), lambda i: (layer, 0, 0))
    return pl.pallas_call(
        _sample_gate_kernel,
        grid=(1,),
        in_specs=[full, full, full, par, par],
        out_specs=full,
        out_shape=jax.ShapeDtypeStruct((n, SSM_D_INNER), bf16),
        compiler_params=_cparams(("arbitrary",)),
        name="sample_gate",
    )(y, xs, z, dsk, ng)


def _paged_attn_kernel(pt_ref, qm_ref, kn_ref, vn_ref, ns_ref, lam_ref, sg_ref, *rest, pps,
                       n_steps, past, lam_init):
    del pt_ref
    k_refs = rest[:pps]
    v_refs = rest[pps:2 * pps]
    o_ref = rest[2 * pps]
    m_ref, l_ref, acc_ref, s_scr = rest[2 * pps + 1:]
    j = pl.program_id(1)
    qm = qm_ref[...]
    nslope = ns_ref[...]

    @pl.when(j == 0)
    def _():
        s0 = _dot(kn_ref[...], qm)
        first = lax.broadcasted_iota(jnp.int32, s0.shape, 0) == 0
        s0 = jnp.where(first, s0, -jnp.inf)
        m0 = jnp.max(s0, axis=0, keepdims=True)
        p0 = jnp.exp2(s0 - m0)
        m_ref[...] = m0
        l_ref[...] = jnp.sum(p0, axis=0, keepdims=True)
        acc_ref[...] = _dot_tn(vn_ref[...], p0.astype(bf16))

    base = j * (pps * PAGE_SIZE)
    key = lax.broadcasted_iota(jnp.int32, (PAGE_SIZE, LANES), 0)
    for i in range(pps):
        s = _dot_tn(k_refs[i][...].astype(bf16), qm)
        dist = (past - (base + i * PAGE_SIZE) - key).astype(f32)
        s_scr[i * PAGE_SIZE:(i + 1) * PAGE_SIZE, :] = s + dist * nslope
    s_all = s_scr[...]
    m_old = m_ref[...]
    m_new = jnp.maximum(m_old, jnp.max(s_all, axis=0, keepdims=True))
    alpha = jnp.exp2(m_old - m_new)
    pe = jnp.exp2(s_all - m_new)
    l_ref[...] = alpha * l_ref[...] + jnp.sum(pe, axis=0, keepdims=True)
    pb = pe.astype(bf16)
    for g in range(ATT_KV_HEADS):
        pv = None
        for i in range(pps):
            vg = v_refs[i][pl.ds(g, PAGE_SIZE, stride=ATT_KV_HEADS), :].astype(bf16)
            t = _dot_tn(vg, pb[i * PAGE_SIZE:(i + 1) * PAGE_SIZE, :])
            pv = t if pv is None else pv + t
        rows = slice(g * ATT_V_DIM, (g + 1) * ATT_V_DIM)
        acc_ref[rows, :] = acc_ref[rows, :] * alpha + pv
    m_ref[...] = m_new

    @pl.when(j == n_steps - 1)
    def _():
        lam = _lambda_value(lam_ref, lam_init)
        inv_l = 1.0 / l_ref[...]
        for g in range(ATT_KV_HEADS):
            ot = (acc_ref[g * ATT_V_DIM:(g + 1) * ATT_V_DIM, :] * inv_l).T
            for r in range(ATT_GROUP):
                c0 = g * (ATT_GROUP * 2) + r * 2
                o = ot[c0:c0 + 1, :] - lam * ot[c0 + 1:c0 + 2, :]
                o = _rms(o) * sg_ref[...] * (1.0 - lam_init)
                h = g * ATT_GROUP + r
                o_ref[:, h * ATT_V_DIM:(h + 1) * ATT_V_DIM] = o.astype(bf16)


def _paged_attn(page_table, qmat, k_new, v_new, nslope, lam_vecs, subln_g, cache_kt, cache_v,
                layer, lam_init, pps):
    n, n_pages = page_table.shape
    n_steps = n_pages // pps
    past = n_pages * PAGE_SIZE

    def k_spec(i):
        return pl.BlockSpec((None, None, ATT_K_COLS, PAGE_SIZE),
                            lambda b, j, pt: (layer, pt[b, j * pps + i], 0, 0))

    def v_spec(i):
        return pl.BlockSpec((None, None, PAGE_SIZE * ATT_KV_HEADS, ATT_V_DIM),
                            lambda b, j, pt: (layer, pt[b, j * pps + i], 0, 0))

    grid_spec = pltpu.PrefetchScalarGridSpec(
        num_scalar_prefetch=1,
        grid=(n, n_steps),
        in_specs=[
            pl.BlockSpec((None, ATT_K_COLS, LANES), lambda b, j, pt: (b, 0, 0)),
            pl.BlockSpec((None, SUBLANES, ATT_K_COLS), lambda b, j, pt: (b, 0, 0)),
            pl.BlockSpec((None, SUBLANES, ATT_V_COLS), lambda b, j, pt: (b, 0, 0)),
            pl.BlockSpec((1, LANES), lambda b, j, pt: (0, 0)),
            pl.BlockSpec((None, 4, ATT_QK_DIM), lambda b, j, pt: (layer, 0, 0)),
            pl.BlockSpec((None, 1, ATT_V_DIM), lambda b, j, pt: (layer, 0, 0)),
        ] + [k_spec(i) for i in range(pps)] + [v_spec(i) for i in range(pps)],
        out_specs=pl.BlockSpec((None, 1, ATT_WIDTH), lambda b, j, pt: (b, 0, 0)),
        scratch_shapes=[pltpu.VMEM((1, LANES), f32), pltpu.VMEM((1, LANES), f32),
                        pltpu.VMEM((ATT_V_COLS, LANES), f32),
                        pltpu.VMEM((pps * PAGE_SIZE, LANES), f32)],
    )
    return pl.pallas_call(
        functools.partial(_paged_attn_kernel, pps=pps, n_steps=n_steps, past=past,
                          lam_init=lam_init),
        grid_spec=grid_spec,
        out_shape=jax.ShapeDtypeStruct((n, 1, ATT_WIDTH), bf16),
        compiler_params=_cparams(("parallel", "arbitrary")),
        name="paged_attn",
    )(page_table, qmat, k_new, v_new, nslope, lam_vecs, subln_g,
      *([cache_kt] * pps), *([cache_v] * pps))


def _alibi_tables():
    import ml_dtypes
    h = np.arange(1, ATT_HEADS + 1, dtype=np.float32)
    slopes = (2.0 ** (-8.0 * h / ATT_HEADS)).astype(np.float32) * np.float32(LOG2E)
    terms = np.zeros((ATT_HEADS, POS_TERMS), np.float32)
    rem = slopes.copy()
    for t in range(POS_TERMS):
        terms[:, t] = rem.astype(ml_dtypes.bfloat16).astype(np.float32)
        rem = rem - terms[:, t]
    per_col = np.zeros((1, LANES), np.float32)
    per_col[0, :ATT_SCORE_COLS] = -np.repeat(slopes, 2)
    return terms.reshape(ATT_KV_HEADS, ATT_GROUP * POS_TERMS), per_col


def _expand_matrix():
    e = np.zeros((LANES, SSM_D_INNER), np.float32)
    for h in range(SSM_HEADS):
        e[h, h * SSM_HEADDIM:(h + 1) * SSM_HEADDIM] = 1.0
    return e


def _pick(n, prefer):
    t = min(prefer, n)
    while n % t:
        t //= 2
    return t


def _query_matrix(q):
    n = q.shape[0]
    q5 = q.reshape(n, ATT_KV_HEADS, ATT_GROUP, 2, ATT_QK_DIM)
    eg = jnp.eye(ATT_KV_HEADS, dtype=q.dtype)
    ec = jnp.eye(2, dtype=q.dtype)
    qm = jnp.einsum('bgrcd,gh,ce->bgcdhre', q5, eg, ec)
    qm = qm.reshape(n, ATT_K_COLS, ATT_SCORE_COLS)
    return jnp.pad(qm, ((0, 0), (0, 0), (0, LANES - ATT_SCORE_COLS)))


def kernel(x_prompt, x_sample, cache_k, cache_v, state_conv, state_ssm, page_table, c_prompt,
           c_sample, w_ada, b_ada, g_pre_mix, g_post_mix, g_pre_ffn, g_post_ffn, w_in, conv_w,
           conv_b, dt_bias, a_log, d_skip, ssm_norm_g, lam_vecs, subln_g, w_branch, w_out, w_gu,
           w_down):
    depth = w_in.shape[0]
    bp, L, _ = x_prompt.shape
    n = x_sample.shape[0]
    n_pool = cache_k.shape[1]
    n_pages = page_table.shape[1]

    z_end = SSM_D_INNER
    xbc_end = z_end + SSM_CONV_DIM
    dt_end = xbc_end + SSM_HEADS
    q_end = dt_end + ATT_Q_COLS
    k_end = q_end + ATT_K_COLS
    v_end = k_end + ATT_V_COLS
    w_dt = jnp.pad(w_in[:, :, xbc_end:dt_end], ((0, 0), (0, 0), (0, DT_PAD - SSM_HEADS)))
    w_q = w_in[:, :, dt_end:q_end] * (ATT_QK_DIM ** -0.5 * LOG2E)
    w_cat = jnp.concatenate(
        [w_in[:, :, :xbc_end], w_q, w_in[:, :, q_end:], w_dt], axis=-1).astype(bf16)
    w_ada_b = w_ada.astype(bf16)
    wb_b = w_branch.astype(bf16)
    wo_b = w_out.astype(bf16)
    wgu_b = w_gu.astype(bf16)
    wd_b = w_down.astype(bf16)
    row = lambda a: a.reshape(depth, 1, a.shape[-1])
    pad_heads = lambda a: jnp.pad(a, ((0, 0), (0, DT_PAD - SSM_HEADS))).reshape(depth, 1, DT_PAD)
    dtb_p = pad_heads(dt_bias)
    alog_p = pad_heads(a_log)
    dsk_f = jnp.repeat(d_skip, SSM_HEADDIM, axis=-1).reshape(depth, 1, SSM_D_INNER)
    g_pre_mix, g_post_mix, g_pre_ffn, g_post_ffn = map(row, (g_pre_mix, g_post_mix, g_pre_ffn,
                                                            g_post_ffn))
    cb_r, ng_r, sg_r = row(conv_b), row(ssm_norm_g), row(subln_g)
    slopes_np, nslope_np = _alibi_tables()
    slopes = jnp.asarray(slopes_np)
    nslope = jnp.asarray(nslope_np)
    e_np = _expand_matrix()
    e_mat = jnp.asarray(e_np, dtype=bf16)
    et_mat = jnp.asarray(e_np.T, dtype=bf16)

    sg_col = subln_g.reshape(depth, ATT_V_DIM, 1)
    cache_kt = jnp.transpose(cache_k, (0, 1, 3, 4, 5, 2)).reshape(depth, n_pool, ATT_K_COLS, PAGE_SIZE)
    cache_vr = cache_v.reshape(depth, n_pool, PAGE_SIZE * ATT_KV_HEADS, ATT_V_DIM)
    state4 = state_ssm.reshape(depth, n, SSM_D_INNER, SSM_D_STATE)

    mod_all = _ada_mod(jnp.concatenate([c_prompt, c_sample], axis=0), w_ada_b, b_ada)

    tl_p = _pick(L, 256)
    tl_wide = _pick(L, 512)
    t_ssd = _pick(L, 128)
    tq = _pick(L, 512)
    pps = _pick(n_pages, PAGES_PER_STEP)

    xp = x_prompt
    xs = x_sample.reshape(1, n, D_MODEL)
    outs = {name: [] for name in ("conv_p", "ssm_p", "conv_s", "ssm_s")}
    kv_p = kv_s = None
    for l in range(depth):
        lam_init = 0.8 - 0.6 * math.exp(-0.3 * l)
        mod_p = mod_all[l, :, :bp].reshape(6, bp, 1, D_MODEL)
        mod_s = mod_all[l, :, bp:].reshape(6, 1, n, D_MODEL)

        z, xbc, q, k_all, v_all, kb, vb, gates, dt = _in_proj(xp, mod_p, g_pre_mix, w_cat, l, depth,
                                                              tl_p, kv_p)
        kv_p = (k_all, v_all)
        y_ssm, st = _ssd_prompt(xbc, z, dt, conv_w, cb_r, dtb_p, alog_p, dsk_f, ng_r, e_mat,
                                et_mat, l, t_ssd)
        y_att = _attn_prompt(q, kb, vb, slopes, lam_vecs, sg_col, l, lam_init, tq,
                             ATT_HEADS_PER_STEP)
        x1 = _merge(y_ssm, y_att, gates, xp, mod_p, g_post_mix, wb_b, wo_b, l, tl_wide)
        xp = _ffn(x1, mod_p, g_pre_ffn, g_post_ffn, wgu_b, wd_b, l, tl_wide)
        outs["conv_p"].append(xbc[:, L - (SSM_CONV - 1):, :])
        outs["ssm_p"].append(st.reshape(bp, SSM_HEADS, SSM_HEADDIM, SSM_D_STATE))

        z, xbc, q, k_all, v_all, kb, vb, gates, dt = _in_proj(xs, mod_s, g_pre_mix, w_cat, l, depth,
                                                              n, kv_s)
        kv_s = (k_all, v_all)
        k_new = k_all[l, 0].T
        v_new = v_all[l, 0].reshape(n, ATT_V_COLS)
        sc_t = jnp.transpose(state_conv[l], (1, 0, 2))
        xs_c, bc, dec3, xdt2, conv_new = _sample_pre(xbc[0], sc_t, dt[0], conv_w, cb_r, dtb_p,
                                                     alog_p, e_mat, l)
        dec_t = jnp.transpose(dec3, (2, 0, 1)).reshape(SSM_D_INNER, 3 * n)
        xdt_t = jnp.transpose(xdt2, (2, 0, 1)).reshape(SSM_D_INNER, 2 * n)
        st_s, y_raw = _sample_state(state4, dec_t, xdt_t, bc, l)
        y_ssm = _sample_gate(y_raw.reshape(n, SSM_D_INNER), xs_c, z[0], dsk_f, ng_r, l)
        qmat = _query_matrix(q[0])
        first_row = lambda a: jnp.pad(a.astype(bf16)[:, None, :],
                                      ((0, 0), (0, SUBLANES - 1), (0, 0)))
        y_att = _paged_attn(page_table, qmat, first_row(k_new), first_row(v_new), nslope, lam_vecs,
                            sg_r, cache_kt, cache_vr, l, lam_init, pps)
        x1 = _merge(y_ssm.reshape(1, n, SSM_D_INNER), y_att.reshape(1, n, ATT_WIDTH), gates, xs,
                    mod_s, g_post_mix, wb_b, wo_b, l, n)
        xs = _ffn(x1, mod_s, g_pre_ffn, g_post_ffn, wgu_b, wd_b, l, n)
        outs["conv_s"].append(jnp.transpose(conv_new, (1, 0, 2)))
        outs["ssm_s"].append(st_s.reshape(n, SSM_HEADS, SSM_HEADDIM, SSM_D_STATE))

    def keys_out(k_all, rows, length):
        k6 = k_all.reshape(depth, rows, ATT_KV_HEADS, 2, ATT_QK_DIM, length)
        return jnp.transpose(k6, (0, 1, 5, 2, 3, 4))

    def values_out(v_all, rows, length):
        return v_all.reshape(depth, rows, length, ATT_KV_HEADS, ATT_V_DIM)

    stack = {name: jnp.stack(o) for name, o in outs.items()}
    return (xp, xs.reshape(n, 1, D_MODEL),
            keys_out(kv_p[0], bp, L), values_out(kv_p[1], bp, L), stack["conv_p"], stack["ssm_p"],
            keys_out(kv_s[0], 1, n).reshape(depth, n, 1, ATT_KV_HEADS, 2, ATT_QK_DIM),
            values_out(kv_s[1], 1, n).reshape(depth, n, 1, ATT_KV_HEADS, ATT_V_DIM),
            stack["conv_s"], stack["ssm_s"])
```

```python
import functools
import math

import numpy as np
import jax
import jax.numpy as jnp
from jax import lax
from jax.experimental import pallas as pl
from jax.experimental.pallas import tpu as pltpu

D_MODEL = 1024
SSM_HEADDIM = 64
SSM_HEADS = 16
SSM_GROUPS = 2
SSM_D_STATE = 128
SSM_CONV = 4
SSM_D_INNER = SSM_HEADS * SSM_HEADDIM
SSM_BC = SSM_GROUPS * SSM_D_STATE
SSM_CONV_DIM = SSM_D_INNER + 2 * SSM_BC
SSM_GROUP_WIDTH = SSM_D_INNER // SSM_GROUPS
ATT_HEADS = 8
ATT_KV_HEADS = 4
ATT_GROUP = ATT_HEADS // ATT_KV_HEADS
ATT_QK_DIM = 64
ATT_V_DIM = 128
ATT_WIDTH = ATT_HEADS * ATT_V_DIM
ATT_Q_COLS = ATT_HEADS * 2 * ATT_QK_DIM
ATT_K_COLS = ATT_KV_HEADS * 2 * ATT_QK_DIM
ATT_V_COLS = ATT_KV_HEADS * ATT_V_DIM
ATT_SCORE_COLS = ATT_KV_HEADS * ATT_GROUP * 2
FFN_HIDDEN = 2816
RMS_EPS = 1e-6
PAGE_SIZE = 128
LANES = 128
SUBLANES = 8
DT_PAD = LANES

_SEG_Z = (0, SSM_D_INNER)
_SEG_XBC = (_SEG_Z[1], _SEG_Z[1] + SSM_CONV_DIM)
_SEG_Q = (_SEG_XBC[1], _SEG_XBC[1] + ATT_Q_COLS)
_SEG_K = (_SEG_Q[1], _SEG_Q[1] + ATT_K_COLS)
_SEG_V = (_SEG_K[1], _SEG_K[1] + ATT_V_COLS)
_SEG_G = (_SEG_V[1], _SEG_V[1] + 2 * D_MODEL)
_SEG_DT = (_SEG_G[1], _SEG_G[1] + DT_PAD)
ATT_KW_COLS = ATT_KV_HEADS * 2 * LANES
IN_COLS_PAD = _SEG_DT[1]
ATT_HEADS_PER_STEP = 4
PAGES_PER_STEP = 32
ONES_ROWS = 2 * SUBLANES
POS_TERMS = 3
LOG2E = math.log2(math.e)

VMEM_LIMIT = 56 * 1024 * 1024

f32 = jnp.float32
bf16 = jnp.bfloat16


def _cparams(sem):
    return pltpu.CompilerParams(dimension_semantics=sem, vmem_limit_bytes=VMEM_LIMIT)


def _rms(x):
    return x * lax.rsqrt(jnp.mean(x * x, axis=-1, keepdims=True) + RMS_EPS)


def _silu(x):
    return x * jax.nn.sigmoid(x)


def _softplus(x):
    return jnp.maximum(x, 0.0) + jnp.log(1.0 + jnp.exp(-jnp.abs(x)))


def _split_bf16(x, terms):
    parts = []
    rem = x
    for _ in range(terms):
        p = rem.astype(bf16)
        parts.append(p)
        rem = rem - p.astype(f32)
    return parts


def _dot(a, b):
    return jnp.dot(a, b, preferred_element_type=f32)


def _dot_nt(a, b):
    return lax.dot_general(a, b, (((1,), (1,)), ((), ())), preferred_element_type=f32)


def _dot_tn(a, b):
    return lax.dot_general(a, b, (((0,), (0,)), ((), ())), preferred_element_type=f32)


def _dot_split(x, w, terms):
    acc = None
    for p in _split_bf16(x, terms):
        t = _dot(p, w)
        acc = t if acc is None else acc + t
    return acc


def _ada_kernel(c_ref, w_ref, b_ref, o_ref):
    s = _silu(c_ref[...])
    o_ref[...] = _dot(s.astype(bf16), w_ref[...]) + b_ref[...]


def _ada_mod(c_all, w_ada_b, b_ada):
    depth = w_ada_b.shape[0]
    rows = c_all.shape[0]
    return pl.pallas_call(
        _ada_kernel,
        grid=(depth, 6),
        in_specs=[
            pl.BlockSpec((rows, D_MODEL), lambda l, j: (0, 0)),
            pl.BlockSpec((None, D_MODEL, D_MODEL), lambda l, j: (l, 0, j)),
            pl.BlockSpec((None, 1, D_MODEL), lambda l, j: (l, 0, j)),
        ],
        out_specs=pl.BlockSpec((None, None, rows, D_MODEL), lambda l, j: (l, j, 0, 0)),
        out_shape=jax.ShapeDtypeStruct((depth, 6, rows, D_MODEL), f32),
        compiler_params=_cparams(("arbitrary", "arbitrary")),
        name="ada_mod",
    )(c_all, w_ada_b, b_ada.reshape(depth, 1, 6 * D_MODEL))


def _inproj_kernel(x_ref, mod_ref, g_ref, w_ref, *rest):
    z_ref, xbc_ref, q_ref, k_ref, v_ref, kb_ref, vb_ref, gt_ref, dt_ref = rest[-9:]
    h = _rms(x_ref[...]) * g_ref[...]
    h = h * (1.0 + mod_ref[1]) + mod_ref[0]
    hb = h.astype(bf16)

    def seg(lo_hi):
        return _dot(hb, w_ref[:, lo_hi[0]:lo_hi[1]])

    z_ref[...] = seg(_SEG_Z)
    xbc_ref[...] = seg(_SEG_XBC)
    q_ref[...] = seg(_SEG_Q).astype(bf16)
    kk = seg(_SEG_K)
    k_ref[...] = kk.T
    rows = x_ref.shape[0]
    lane = lax.broadcasted_iota(jnp.int32, (rows, LANES), 1)
    pos = pl.program_id(1) * rows + lax.broadcasted_iota(jnp.int32, (rows, LANES), 0)
    part = jnp.where((lane & 1) == 0, ((pos >> 8) << 8).astype(f32), (pos & 255).astype(f32))
    off = lane & (ATT_QK_DIM - 1)
    extra = jnp.where(off < 2 * POS_TERMS, part, 0.0)
    for g in range(ATT_KV_HEADS):
        kv = kk[:, g * LANES:(g + 1) * LANES]
        kb_ref[:, 2 * g * LANES:(2 * g + 1) * LANES] = jnp.where(
            lane < ATT_QK_DIM, kv, extra).astype(bf16)
        kb_ref[:, (2 * g + 1) * LANES:(2 * g + 2) * LANES] = jnp.where(
            lane >= ATT_QK_DIM, kv, extra).astype(bf16)
    vv = seg(_SEG_V)
    for g in range(ATT_KV_HEADS):
        v_ref[pl.ds(g, rows, stride=ATT_KV_HEADS), :] = vv[:, g * ATT_V_DIM:(g + 1) * ATT_V_DIM]
    vb_ref[...] = vv.T.astype(bf16)
    gt_ref[...] = seg(_SEG_G)
    dt_ref[...] = seg(_SEG_DT)


K_OUT, V_OUT, VT_OUT = 3, 4, 6


def _in_proj(x, mod, g_pre, w_cat, layer, depth, tl, kv_buffers=None):
    nb, L, _ = x.shape
    r = mod.shape[2]
    widths = [(SSM_D_INNER, f32), (SSM_CONV_DIM, f32), (ATT_Q_COLS, bf16), (ATT_K_COLS, f32),
              (ATT_V_COLS, f32), (ATT_KW_COLS, bf16), (ATT_V_COLS, bf16), (2 * D_MODEL, f32),
              (DT_PAD, f32)]
    tok = lambda w: pl.BlockSpec((None, tl, w), lambda b, i: (b, i, 0))
    out_specs = [tok(w) for w, _ in widths]
    out_shape = [jax.ShapeDtypeStruct((nb, L, w), dt) for w, dt in widths]
    out_specs[VT_OUT] = pl.BlockSpec((None, ATT_V_COLS, tl), lambda b, i: (b, 0, i))
    out_shape[VT_OUT] = jax.ShapeDtypeStruct((nb, ATT_V_COLS, L), bf16)
    out_specs[K_OUT] = pl.BlockSpec((None, None, ATT_K_COLS, tl), lambda b, i: (layer, b, 0, i))
    out_shape[K_OUT] = jax.ShapeDtypeStruct((depth, nb, ATT_K_COLS, L), f32)
    out_specs[V_OUT] = pl.BlockSpec((None, None, tl * ATT_KV_HEADS, ATT_V_DIM),
                                    lambda b, i: (layer, b, i, 0))
    out_shape[V_OUT] = jax.ShapeDtypeStruct((depth, nb, L * ATT_KV_HEADS, ATT_V_DIM), f32)
    in_specs = [
        tok(D_MODEL),
        pl.BlockSpec((6, None, r, D_MODEL), lambda b, i: (0, b, 0, 0)),
        pl.BlockSpec((None, 1, D_MODEL), lambda b, i: (layer, 0, 0)),
        pl.BlockSpec((None, D_MODEL, IN_COLS_PAD), lambda b, i: (layer, 0, 0),
                     pipeline_mode=pl.Buffered(1)),
    ]
    args = [x, mod, g_pre, w_cat]
    aliases = {}
    if kv_buffers is not None:
        aliases = {len(args): K_OUT, len(args) + 1: V_OUT}
        in_specs += [pl.BlockSpec(memory_space=pl.ANY), pl.BlockSpec(memory_space=pl.ANY)]
        args += list(kv_buffers)
    return pl.pallas_call(
        _inproj_kernel,
        grid=(nb, L // tl),
        in_specs=in_specs,
        out_specs=out_specs,
        out_shape=out_shape,
        input_output_aliases=aliases,
        compiler_params=_cparams(("parallel", "parallel")),
        name="in_proj",
    )(*args)


def _ssd_kernel(xbc_ref, z_ref, dt_ref, cw_ref, cb_ref, dtb_ref, alog_ref, dsk_ref, ng_ref, e_ref,
                y_ref, st_ref, xpad_ref, *, T):
    c = pl.program_id(1)

    @pl.when(c == 0)
    def _():
        xpad_ref[0:SUBLANES, :] = jnp.zeros((SUBLANES, SSM_CONV_DIM), f32)
        st_ref[...] = jnp.zeros_like(st_ref)

    xr = xbc_ref[...]
    xpad_ref[SUBLANES:SUBLANES + T, :] = xr
    acc = xr * cw_ref[SSM_CONV - 1:SSM_CONV, :]
    for lag in range(1, SSM_CONV):
        acc = acc + (xpad_ref[SUBLANES - lag:SUBLANES - lag + T, :]
                     * cw_ref[SSM_CONV - 1 - lag:SSM_CONV - lag, :])
    xpad_ref[0:SUBLANES, :] = xpad_ref[T:T + SUBLANES, :]
    conv = _silu(acc + cb_ref[...])
    xs = conv[:, :SSM_D_INNER]
    bm = conv[:, SSM_D_INNER:SSM_D_INNER + SSM_BC].astype(bf16)
    cm = conv[:, SSM_D_INNER + SSM_BC:].astype(bf16)

    dt = _softplus(dt_ref[...] + dtb_ref[...])
    da = dt * (-jnp.exp(alog_ref[...]))
    row = lax.broadcasted_iota(jnp.int32, (T, T), 0)
    col = lax.broadcasted_iota(jnp.int32, (T, T), 1)
    causal = row >= col
    tri = causal.astype(bf16)
    acum = None
    for p in _split_bf16(da, 3):
        t = _dot(tri, p)
        acum = t if acum is None else acum + t
    acum_t = acum.T
    last = acum[T - 1:T, :]
    ea = jnp.exp(acum)
    to_end = jnp.exp(last - acum)

    e = e_ref[...]
    dt_f = _dot_split(dt, e, 2)
    ea_f = _dot_split(ea, e, 2)
    te_f = _dot_split(to_end, e, 2)
    xdt = xs * dt_f
    xdt_b = xdt.astype(bf16)
    xdtw_b = (xdt * te_f).astype(bf16)

    cd = jnp.exp(jnp.broadcast_to(acum_t[:, T - 1:T], (LANES, SSM_D_STATE)))

    hpg = SSM_HEADS // SSM_GROUPS
    pair_lane = lax.broadcasted_iota(jnp.int32, (T, 2 * SSM_HEADDIM), 1)
    y_parts = []
    for g in range(SSM_GROUPS):
        bg = bm[:, g * SSM_D_STATE:(g + 1) * SSM_D_STATE]
        cg = cm[:, g * SSM_D_STATE:(g + 1) * SSM_D_STATE]
        cb = _dot_nt(cg, bg)
        rows = slice(g * SSM_GROUP_WIDTH, (g + 1) * SSM_GROUP_WIDTH)
        s_in = st_ref[rows, :]
        y_off = _dot_nt(cg, s_in.astype(bf16)) * ea_f[:, rows]

        def decay_mix(h):
            seg = acum[:, h:h + 1] - acum_t[h:h + 1, :]
            dec = jnp.exp(jnp.where(causal, seg, -jnp.inf))
            return (cb * dec).astype(bf16)

        diag = []
        for hh in range(0, hpg, 2):
            h = g * hpg + hh
            xpair = xdt_b[:, h * SSM_HEADDIM:(h + 2) * SSM_HEADDIM]
            zero = jnp.zeros_like(xpair)
            w2 = jnp.concatenate([jnp.where(pair_lane < SSM_HEADDIM, xpair, zero),
                                  jnp.where(pair_lane >= SSM_HEADDIM, xpair, zero)], axis=0)
            m2 = jnp.concatenate([decay_mix(h), decay_mix(h + 1)], axis=1)
            diag.append(_dot(m2, w2))
        y_parts.append(jnp.concatenate(diag, axis=1) + y_off)
        st_new = _dot_tn(xdtw_b[:, rows], bg)
        for hh in range(hpg):
            h = g * hpg + hh
            hrows = slice(h * SSM_HEADDIM, (h + 1) * SSM_HEADDIM)
            lrows = slice(hh * SSM_HEADDIM, (hh + 1) * SSM_HEADDIM)
            st_ref[hrows, :] = s_in[lrows, :] * cd[h:h + 1, :] + st_new[lrows, :]
    y = jnp.concatenate(y_parts, axis=1) + dsk_ref[...] * xs
    yz = y * _silu(z_ref[...])
    outs = []
    for g in range(SSM_GROUPS):
        outs.append(_rms(yz[:, g * SSM_GROUP_WIDTH:(g + 1) * SSM_GROUP_WIDTH]))
    y_ref[...] = (jnp.concatenate(outs, axis=1) * ng_ref[...]).astype(bf16)


def _ssd_prompt(xbc, z, dt, cw, cb, dtb, alog, dsk, ng, e, layer, T):
    nb, L, _ = xbc.shape
    tok = lambda w: pl.BlockSpec((None, T, w), lambda b, i: (b, i, 0))
    par = lambda r, w: pl.BlockSpec((None, r, w), lambda b, i: (layer, 0, 0))
    return pl.pallas_call(
        functools.partial(_ssd_kernel, T=T),
        grid=(nb, L // T),
        in_specs=[tok(SSM_CONV_DIM), tok(SSM_D_INNER), tok(DT_PAD),
                  par(SSM_CONV, SSM_CONV_DIM), par(1, SSM_CONV_DIM), par(1, DT_PAD), par(1, DT_PAD),
                  par(1, SSM_D_INNER), par(1, SSM_D_INNER),
                  pl.BlockSpec((LANES, SSM_D_INNER), lambda b, i: (0, 0))],
        out_specs=[tok(SSM_D_INNER),
                   pl.BlockSpec((None, SSM_D_INNER, SSM_D_STATE), lambda b, i: (b, 0, 0))],
        out_shape=[jax.ShapeDtypeStruct((nb, L, SSM_D_INNER), bf16),
                   jax.ShapeDtypeStruct((nb, SSM_D_INNER, SSM_D_STATE), f32)],
        scratch_shapes=[pltpu.VMEM((T + SUBLANES, SSM_CONV_DIM), f32)],
        compiler_params=_cparams(("parallel", "arbitrary")),
        name="ssd_prompt",
    )(xbc, z, dt, cw, cb, dtb, alog, dsk, ng, e)


def _lambda_value(lam_ref, lam_init):
    lv = lam_ref[...]
    a = jnp.sum(lv[0:1, :] * lv[1:2, :], axis=-1, keepdims=True)
    b = jnp.sum(lv[2:3, :] * lv[3:4, :], axis=-1, keepdims=True)
    return jnp.exp(a) - jnp.exp(b) + lam_init


def _sum_row_groups(x):
    acc = x[0:SUBLANES, :]
    for i in range(1, x.shape[0] // SUBLANES):
        acc = acc + x[i * SUBLANES:(i + 1) * SUBLANES, :]
    return acc


def _attn_kernel(qi_ref, ki_ref, q_ref, k_ref, v_ref, slope_ref, lam_ref, sgc_ref, o_ref,
                 m_ref, acc_ref, qa_ref, s_ref, p_ref, *, tq, tk, lam_init, gu):
    g0 = pl.program_id(1) * gu
    step = pl.program_id(2)
    qi = qi_ref[step]
    ki = ki_ref[step]
    half = ATT_QK_DIM
    qw = ATT_GROUP * 2 * half
    kw = 2 * LANES
    dv = ATT_V_DIM

    @pl.when(ki == 0)
    def _():
        m_ref[...] = jnp.full_like(m_ref, -jnp.inf)
        acc_ref[...] = jnp.zeros_like(acc_ref)
        lane = lax.broadcasted_iota(jnp.int32, (tq, 2 * half), 1)
        for gl in range(gu):
            for r in range(ATT_GROUP):
                qv = q_ref[:, gl * qw + r * 2 * half:gl * qw + (r + 1) * 2 * half]
                a0 = jnp.zeros((tq, 2 * half), f32)
                a1 = jnp.zeros((tq, 2 * half), f32)
                for t in range(POS_TERMS):
                    term = slope_ref[g0 + gl, r * POS_TERMS + t]
                    a0 = jnp.where((lane >> 1) == (half >> 1) + t, term, a0)
                    a1 = jnp.where((lane >> 1) == t, term, a1)
                qa_ref[4 * gl + 2 * r] = jnp.where(lane < half, qv, a0.astype(bf16))
                qa_ref[4 * gl + 2 * r + 1] = jnp.where(lane >= half, qv, a1.astype(bf16))

    def run(masked, gl):
        ones = jnp.ones((ONES_ROWS, tk), bf16)
        v1 = jnp.concatenate([v_ref[gl * dv:(gl + 1) * dv, :], ones], axis=0)
        ka = (k_ref[:, gl * kw:gl * kw + 2 * half], k_ref[:, gl * kw + 2 * half:(gl + 1) * kw])
        if masked:
            keep = (lax.broadcasted_iota(jnp.int32, (tk, tq), 0)
                    <= lax.broadcasted_iota(jnp.int32, (tk, tq), 1))
        alphas = []
        for c in range(4):
            idx = 4 * gl + c
            s = _dot_nt(ka[c % 2], qa_ref[idx])
            if masked:
                s = jnp.where(keep, s, -jnp.inf)
            s_ref[idx] = s
            m_old = m_ref[idx]
            m_new = jnp.maximum(m_old, jnp.max(s, axis=0, keepdims=True))
            alphas.append(jnp.exp2(m_old - m_new))
            m_ref[idx] = m_new
        grp = 2 * SUBLANES
        for c in range(4):
            idx = 4 * gl + c
            mg = jnp.broadcast_to(m_ref[idx], (grp, tq))
            for i in range(tk // grp):
                rows = slice(i * grp, (i + 1) * grp)
                p_ref[idx, rows, :] = jnp.exp2(s_ref[idx, rows, :] - mg).astype(bf16)
        for c in range(4):
            idx = 4 * gl + c
            acc_ref[idx] = alphas[c] * acc_ref[idx] + _dot(v1, p_ref[idx])

    def finish(gl):
        lam = _lambda_value(lam_ref, lam_init)
        for r in range(ATT_GROUP):
            i0 = 4 * gl + 2 * r
            o = (acc_ref[i0, :dv, :] / acc_ref[i0, dv:dv + 1, :]
                 - lam * (acc_ref[i0 + 1, :dv, :] / acc_ref[i0 + 1, dv:dv + 1, :]))
            o = o * lax.rsqrt(jnp.mean(o * o, axis=0, keepdims=True) + RMS_EPS)
            o = o * sgc_ref[...] * (1.0 - lam_init)
            col = (gl * ATT_GROUP + r) * dv
            o_ref[:, col:col + dv] = o.T.astype(bf16)

    @pl.when(ki < qi)
    def _():
        for gl in range(gu):
            run(False, gl)

    @pl.when(ki == qi)
    def _():
        for gl in range(gu):
            run(True, gl)
            finish(gl)


def _attn_prompt(q, kb, vt, slopes, lam_vecs, subln_col, layer, lam_init, tq, gu):
    nb, L, _ = q.shape
    assert L <= 65536, "key positions are split into two 8-bit parts"
    nq = L // tq
    pairs = [(i, j) for i in range(nq) for j in range(i + 1)]
    qi_tab = jnp.asarray(np.array([a for a, _ in pairs], np.int32))
    ki_tab = jnp.asarray(np.array([b for _, b in pairs], np.int32))
    qw = gu * ATT_GROUP * 2 * ATT_QK_DIM
    kw = gu * 2 * LANES
    ow = gu * ATT_GROUP * ATT_V_DIM
    grid_spec = pltpu.PrefetchScalarGridSpec(
        num_scalar_prefetch=2,
        grid=(nb, ATT_KV_HEADS // gu, len(pairs)),
        in_specs=[
            pl.BlockSpec((None, tq, qw), lambda b, g, p, qi, ki: (b, qi[p], g)),
            pl.BlockSpec((None, tq, kw), lambda b, g, p, qi, ki: (b, ki[p], g)),
            pl.BlockSpec((None, gu * ATT_V_DIM, tq), lambda b, g, p, qi, ki: (b, g, ki[p])),
            pl.BlockSpec(memory_space=pltpu.SMEM),
            pl.BlockSpec((None, 4, ATT_QK_DIM), lambda b, g, p, qi, ki: (layer, 0, 0)),
            pl.BlockSpec((None, ATT_V_DIM, 1), lambda b, g, p, qi, ki: (layer, 0, 0)),
        ],
        out_specs=pl.BlockSpec((None, tq, ow), lambda b, g, p, qi, ki: (b, qi[p], g)),
        scratch_shapes=[pltpu.VMEM((4 * gu, 1, tq), f32),
                        pltpu.VMEM((4 * gu, ATT_V_DIM + ONES_ROWS, tq), f32),
                        pltpu.VMEM((4 * gu, tq, 2 * ATT_QK_DIM), bf16),
                        pltpu.VMEM((4 * gu, tq, tq), f32), pltpu.VMEM((4 * gu, tq, tq), bf16)],
    )
    return pl.pallas_call(
        functools.partial(_attn_kernel, tq=tq, tk=tq, lam_init=lam_init, gu=gu),
        grid_spec=grid_spec,
        out_shape=jax.ShapeDtypeStruct((nb, L, ATT_WIDTH), bf16),
        compiler_params=_cparams(("parallel", "parallel", "arbitrary")),
        name="attn_prompt",
    )(qi_tab, ki_tab, q, kb, vt, slopes, lam_vecs, subln_col)


def _merge_kernel(ys_ref, ya_ref, gt_ref, x_ref, mod_ref, g_ref, wb_ref, wo_ref, o_ref):
    bp0 = _dot(ys_ref[...], wb_ref[0])
    bp1 = _dot(ya_ref[...], wb_ref[1])
    gt = gt_ref[...]
    merged = (jax.nn.sigmoid(gt[:, :D_MODEL]) * bp0 + jax.nn.sigmoid(gt[:, D_MODEL:]) * bp1)
    m = _dot(merged.astype(bf16), wo_ref[...])
    o_ref[...] = x_ref[...] + mod_ref[2] * (_rms(m) * g_ref[...])


def _merge(y_ssm, y_att, gates, x, mod, g_post, wb, wo, layer, tl):
    nb, L, _ = x.shape
    r = mod.shape[2]
    tok = lambda w: pl.BlockSpec((None, tl, w), lambda b, i: (b, i, 0))
    return pl.pallas_call(
        _merge_kernel,
        grid=(nb, L // tl),
        in_specs=[
            tok(SSM_D_INNER), tok(ATT_WIDTH), tok(2 * D_MODEL), tok(D_MODEL),
            pl.BlockSpec((6, None, r, D_MODEL), lambda b, i: (0, b, 0, 0)),
            pl.BlockSpec((None, 1, D_MODEL), lambda b, i: (layer, 0, 0)),
            pl.BlockSpec((None, 2, ATT_WIDTH, D_MODEL), lambda b, i: (layer, 0, 0, 0),
                         pipeline_mode=pl.Buffered(1)),
            pl.BlockSpec((None, D_MODEL, D_MODEL), lambda b, i: (layer, 0, 0),
                         pipeline_mode=pl.Buffered(1)),
        ],
        out_specs=tok(D_MODEL),
        out_shape=jax.ShapeDtypeStruct((nb, L, D_MODEL), f32),
        compiler_params=_cparams(("parallel", "parallel")),
        name="merge_out",
    )(y_ssm, y_att, gates, x, mod, g_post, wb, wo)


def _ffn_kernel(x_ref, mod_ref, g1_ref, g2_ref, wgu_ref, wd_ref, o_ref):
    x = x_ref[...]
    h = _rms(x) * g1_ref[...]
    hb = (h * (1.0 + mod_ref[4]) + mod_ref[3]).astype(bf16)
    gg = _dot(hb, wgu_ref[:, :FFN_HIDDEN])
    uu = _dot(hb, wgu_ref[:, FFN_HIDDEN:])
    f = _dot((_silu(gg) * uu).astype(bf16), wd_ref[...])
    o_ref[...] = x + mod_ref[5] * (_rms(f) * g2_ref[...])


def _ffn(x, mod, g_pre, g_post, wgu, wd, layer, tl):
    nb, L, _ = x.shape
    r = mod.shape[2]
    tok = pl.BlockSpec((None, tl, D_MODEL), lambda b, i: (b, i, 0))
    par = pl.BlockSpec((None, 1, D_MODEL), lambda b, i: (layer, 0, 0))
    return pl.pallas_call(
        _ffn_kernel,
        grid=(nb, L // tl),
        in_specs=[
            tok,
            pl.BlockSpec((6, None, r, D_MODEL), lambda b, i: (0, b, 0, 0)),
            par, par,
            pl.BlockSpec((None, D_MODEL, 2 * FFN_HIDDEN), lambda b, i: (layer, 0, 0),
                         pipeline_mode=pl.Buffered(1)),
            pl.BlockSpec((None, FFN_HIDDEN, D_MODEL), lambda b, i: (layer, 0, 0),
                         pipeline_mode=pl.Buffered(1)),
        ],
        out_specs=tok,
        out_shape=jax.ShapeDtypeStruct((nb, L, D_MODEL), f32),
        compiler_params=_cparams(("parallel", "parallel")),
        name="ffn",
    )(x, mod, g_pre, g_post, wgu, wd)


def _sample_pre_kernel(xbc_ref, sc_ref, dt_ref, cw_ref, cb_ref, dtb_ref, alog_ref, e_ref,
                       xs_ref, bc_ref, dec_ref, xdt_ref, cn_ref):
    xr = xbc_ref[...]
    acc = xr * cw_ref[SSM_CONV - 1:SSM_CONV, :]
    for j in range(SSM_CONV - 1):
        acc = acc + sc_ref[j] * cw_ref[j:j + 1, :]
    conv = _silu(acc + cb_ref[...])
    xs = conv[:, :SSM_D_INNER]
    xs_ref[...] = xs
    bc_ref[...] = conv[:, SSM_D_INNER:]
    for j in range(SSM_CONV - 2):
        cn_ref[j] = sc_ref[j + 1]
    cn_ref[SSM_CONV - 2] = xr
    dt = _softplus(dt_ref[...] + dtb_ref[...])
    da = dt * (-jnp.exp(alog_ref[...]))
    e = e_ref[...]
    dec = jnp.exp(_dot_split(da, e, 3))
    xdt = xs * _dot_split(dt, e, 3)
    for j, p in enumerate(_split_bf16(dec, 3)):
        dec_ref[j] = p
    for j, p in enumerate(_split_bf16(xdt, 2)):
        xdt_ref[j] = p


def _sample_pre(xbc, sc_t, dt, cw, cb, dtb, alog, e, layer):
    n = xbc.shape[0]
    full = lambda *s: pl.BlockSpec(s, lambda i: (0,) * len(s))
    par = lambda r, w: pl.BlockSpec((None, r, w), lambda i: (layer, 0, 0))
    return pl.pallas_call(
        _sample_pre_kernel,
        grid=(1,),
        in_specs=[full(n, SSM_CONV_DIM), full(SSM_CONV - 1, n, SSM_CONV_DIM), full(n, DT_PAD),
                  par(SSM_CONV, SSM_CONV_DIM), par(1, SSM_CONV_DIM), par(1, DT_PAD), par(1, DT_PAD),
                  full(LANES, SSM_D_INNER)],
        out_specs=[full(n, SSM_D_INNER), full(n, 2 * SSM_BC), full(3, n, SSM_D_INNER),
                   full(2, n, SSM_D_INNER), full(SSM_CONV - 1, n, SSM_CONV_DIM)],
        out_shape=[jax.ShapeDtypeStruct((n, SSM_D_INNER), f32),
                   jax.ShapeDtypeStruct((n, 2 * SSM_BC), f32),
                   jax.ShapeDtypeStruct((3, n, SSM_D_INNER), bf16),
                   jax.ShapeDtypeStruct((2, n, SSM_D_INNER), bf16),
                   jax.ShapeDtypeStruct((SSM_CONV - 1, n, SSM_CONV_DIM), f32)],
        compiler_params=_cparams(("arbitrary",)),
        name="sample_pre",
    )(xbc, sc_t, dt, cw, cb, dtb, alog, e)


def _sample_state_kernel(s_ref, dec_ref, xdt_ref, bc_ref, so_ref, y_ref, *, n):
    b = pl.program_id(0)

    def onehot(copies):
        rr = lax.broadcasted_iota(jnp.int32, (copies * n, SSM_D_STATE), 0)
        hit = rr == b
        for c in range(1, copies):
            hit = jnp.logical_or(hit, rr == b + c * n)
        return hit.astype(bf16)

    dm = _dot(dec_ref[...], onehot(3))
    xm = _dot(xdt_ref[...], onehot(2))
    brow = bc_ref[pl.ds(b, 1), :]
    y_parts = []
    for g in range(SSM_GROUPS):
        rows = slice(g * SSM_GROUP_WIDTH, (g + 1) * SSM_GROUP_WIDTH)
        bg = brow[:, g * SSM_D_STATE:(g + 1) * SSM_D_STATE]
        cg = brow[:, SSM_BC + g * SSM_D_STATE:SSM_BC + (g + 1) * SSM_D_STATE]
        s_new = s_ref[rows, :] * dm[rows, :] + xm[rows, :] * bg
        so_ref[rows, :] = s_new
        c8 = jnp.broadcast_to(cg, (SUBLANES, SSM_D_STATE)).astype(bf16)
        y_parts.append(_dot_nt(c8, s_new.astype(bf16))[0:1, :])
    y_ref[...] = jnp.concatenate(y_parts, axis=1)


def _sample_state(state, dec_t, xdt_t, bc, layer):
    n = bc.shape[0]
    return pl.pallas_call(
        functools.partial(_sample_state_kernel, n=n),
        grid=(n,),
        in_specs=[
            pl.BlockSpec((None, None, SSM_D_INNER, SSM_D_STATE), lambda b: (layer, b, 0, 0)),
            pl.BlockSpec((SSM_D_INNER, 3 * n), lambda b: (0, 0)),
            pl.BlockSpec((SSM_D_INNER, 2 * n), lambda b: (0, 0)),
            pl.BlockSpec((n, 2 * SSM_BC), lambda b: (0, 0)),
        ],
        out_specs=[pl.BlockSpec((None, SSM_D_INNER, SSM_D_STATE), lambda b: (b, 0, 0)),
                   pl.BlockSpec((None, 1, SSM_D_INNER), lambda b: (b, 0, 0))],
        out_shape=[jax.ShapeDtypeStruct((n, SSM_D_INNER, SSM_D_STATE), f32),
                   jax.ShapeDtypeStruct((n, 1, SSM_D_INNER), f32)],
        compiler_params=_cparams(("parallel",)),
        name="sample_state",
    )(state, dec_t, xdt_t, bc)


def _sample_gate_kernel(y_ref, xs_ref, z_ref, dsk_ref, ng_ref, o_ref):
    y = y_ref[...] + dsk_ref[...] * xs_ref[...]
    yz = y * _silu(z_ref[...])
    outs = []
    for g in range(SSM_GROUPS):
        outs.append(_rms(yz[:, g * SSM_GROUP_WIDTH:(g + 1) * SSM_GROUP_WIDTH]))
    o_ref[...] = (jnp.concatenate(outs, axis=1) * ng_ref[...]).astype(bf16)


def _sample_gate(y, xs, z, dsk, ng, layer):
    n = y.shape[0]
    full = pl.BlockSpec((n, SSM_D_INNER), lambda i: (0, 0))
    par = pl.BlockSpec((None, 1, SSM_D_INNER), lambda i: (layer, 0, 0))
    return pl.pallas_call(
        _sample_gate_kernel,
        grid=(1,),
        in_specs=[full, full, full, par, par],
        out_specs=full,
        out_shape=jax.ShapeDtypeStruct((n, SSM_D_INNER), bf16),
        compiler_params=_cparams(("arbitrary",)),
        name="sample_gate",
    )(y, xs, z, dsk, ng)


def _paged_attn_kernel(pt_ref, qm_ref, kn_ref, vn_ref, ns_ref, lam_ref, sg_ref, *rest, pps,
                       n_steps, past, lam_init):
    del pt_ref
    k_refs = rest[:pps]
    v_refs = rest[pps:2 * pps]
    o_ref = rest[2 * pps]
    m_ref, l_ref, acc_ref, s_scr = rest[2 * pps + 1:]
    j = pl.program_id(1)
    qm = qm_ref[...]
    nslope = ns_ref[...]

    @pl.when(j == 0)
    def _():
        s0 = _dot(kn_ref[...], qm)
        first = lax.broadcasted_iota(jnp.int32, s0.shape, 0) == 0
        s0 = jnp.where(first, s0, -jnp.inf)
        m0 = jnp.max(s0, axis=0, keepdims=True)
        p0 = jnp.exp2(s0 - m0)
        m_ref[...] = m0
        l_ref[...] = jnp.sum(p0, axis=0, keepdims=True)
        acc_ref[...] = _dot_tn(vn_ref[...], p0.astype(bf16))

    base = j * (pps * PAGE_SIZE)
    key = lax.broadcasted_iota(jnp.int32, (PAGE_SIZE, LANES), 0)
    for i in range(pps):
        s = _dot_tn(k_refs[i][...].astype(bf16), qm)
        dist = (past - (base + i * PAGE_SIZE) - key).astype(f32)
        s_scr[i * PAGE_SIZE:(i + 1) * PAGE_SIZE, :] = s + dist * nslope
    s_all = s_scr[...]
    m_old = m_ref[...]
    m_new = jnp.maximum(m_old, jnp.max(s_all, axis=0, keepdims=True))
    alpha = jnp.exp2(m_old - m_new)
    pe = jnp.exp2(s_all - m_new)
    l_ref[...] = alpha * l_ref[...] + jnp.sum(pe, axis=0, keepdims=True)
    pb = pe.astype(bf16)
    for g in range(ATT_KV_HEADS):
        pv = None
        for i in range(pps):
            vg = v_refs[i][pl.ds(g, PAGE_SIZE, stride=ATT_KV_HEADS), :].astype(bf16)
            t = _dot_tn(vg, pb[i * PAGE_SIZE:(i + 1) * PAGE_SIZE, :])
            pv = t if pv is None else pv + t
        rows = slice(g * ATT_V_DIM, (g + 1) * ATT_V_DIM)
        acc_ref[rows, :] = acc_ref[rows, :] * alpha + pv
    m_ref[...] = m_new

    @pl.when(j == n_steps - 1)
    def _():
        lam = _lambda_value(lam_ref, lam_init)
        inv_l = 1.0 / l_ref[...]
        for g in range(ATT_KV_HEADS):
            ot = (acc_ref[g * ATT_V_DIM:(g + 1) * ATT_V_DIM, :] * inv_l).T
            for r in range(ATT_GROUP):
                c0 = g * (ATT_GROUP * 2) + r * 2
                o = ot[c0:c0 + 1, :] - lam * ot[c0 + 1:c0 + 2, :]
                o = _rms(o) * sg_ref[...] * (1.0 - lam_init)
                h = g * ATT_GROUP + r
                o_ref[:, h * ATT_V_DIM:(h + 1) * ATT_V_DIM] = o.astype(bf16)


def _paged_attn(page_table, qmat, k_new, v_new, nslope, lam_vecs, subln_g, cache_kt, cache_v,
                layer, lam_init, pps):
    n, n_pages = page_table.shape
    n_steps = n_pages // pps
    past = n_pages * PAGE_SIZE

    def k_spec(i):
        return pl.BlockSpec((None, None, ATT_K_COLS, PAGE_SIZE),
                            lambda b, j, pt: (layer, pt[b, j * pps + i], 0, 0))

    def v_spec(i):
        return pl.BlockSpec((None, None, PAGE_SIZE * ATT_KV_HEADS, ATT_V_DIM),
                            lambda b, j, pt: (layer, pt[b, j * pps + i], 0, 0))

    grid_spec = pltpu.PrefetchScalarGridSpec(
        num_scalar_prefetch=1,
        grid=(n, n_steps),
        in_specs=[
            pl.BlockSpec((None, ATT_K_COLS, LANES), lambda b, j, pt: (b, 0, 0)),
            pl.BlockSpec((None, SUBLANES, ATT_K_COLS), lambda b, j, pt: (b, 0, 0)),
            pl.BlockSpec((None, SUBLANES, ATT_V_COLS), lambda b, j, pt: (b, 0, 0)),
            pl.BlockSpec((1, LANES), lambda b, j, pt: (0, 0)),
            pl.BlockSpec((None, 4, ATT_QK_DIM), lambda b, j, pt: (layer, 0, 0)),
            pl.BlockSpec((None, 1, ATT_V_DIM), lambda b, j, pt: (layer, 0, 0)),
        ] + [k_spec(i) for i in range(pps)] + [v_spec(i) for i in range(pps)],
        out_specs=pl.BlockSpec((None, 1, ATT_WIDTH), lambda b, j, pt: (b, 0, 0)),
        scratch_shapes=[pltpu.VMEM((1, LANES), f32), pltpu.VMEM((1, LANES), f32),
                        pltpu.VMEM((ATT_V_COLS, LANES), f32),
                        pltpu.VMEM((pps * PAGE_SIZE, LANES), f32)],
    )
    return pl.pallas_call(
        functools.partial(_paged_attn_kernel, pps=pps, n_steps=n_steps, past=past,
                          lam_init=lam_init),
        grid_spec=grid_spec,
        out_shape=jax.ShapeDtypeStruct((n, 1, ATT_WIDTH), bf16),
        compiler_params=_cparams(("parallel", "arbitrary")),
        name="paged_attn",
    )(page_table, qmat, k_new, v_new, nslope, lam_vecs, subln_g,
      *([cache_kt] * pps), *([cache_v] * pps))


def _alibi_tables():
    import ml_dtypes
    h = np.arange(1, ATT_HEADS + 1, dtype=np.float32)
    slopes = (2.0 ** (-8.0 * h / ATT_HEADS)).astype(np.float32) * np.float32(LOG2E)
    terms = np.zeros((ATT_HEADS, POS_TERMS), np.float32)
    rem = slopes.copy()
    for t in range(POS_TERMS):
        terms[:, t] = rem.astype(ml_dtypes.bfloat16).astype(np.float32)
        rem = rem - terms[:, t]
    per_col = np.zeros((1, LANES), np.float32)
    per_col[0, :ATT_SCORE_COLS] = -np.repeat(slopes, 2)
    return terms.reshape(ATT_KV_HEADS, ATT_GROUP * POS_TERMS), per_col


def _expand_matrix():
    e = np.zeros((LANES, SSM_D_INNER), np.float32)
    for h in range(SSM_HEADS):
        e[h, h * SSM_HEADDIM:(h + 1) * SSM_HEADDIM] = 1.0
    return e


def _pick(n, prefer):
    t = min(prefer, n)
    while n % t:
        t //= 2
    return t


def _query_matrix(q):
    n = q.shape[0]
    q5 = q.reshape(n, ATT_KV_HEADS, ATT_GROUP, 2, ATT_QK_DIM)
    eg = jnp.eye(ATT_KV_HEADS, dtype=q.dtype)
    ec = jnp.eye(2, dtype=q.dtype)
    qm = jnp.einsum('bgrcd,gh,ce->bgcdhre', q5, eg, ec)
    qm = qm.reshape(n, ATT_K_COLS, ATT_SCORE_COLS)
    return jnp.pad(qm, ((0, 0), (0, 0), (0, LANES - ATT_SCORE_COLS)))


def kernel(x_prompt, x_sample, cache_k, cache_v, state_conv, state_ssm, page_table, c_prompt,
           c_sample, w_ada, b_ada, g_pre_mix, g_post_mix, g_pre_ffn, g_post_ffn, w_in, conv_w,
           conv_b, dt_bias, a_log, d_skip, ssm_norm_g, lam_vecs, subln_g, w_branch, w_out, w_gu,
           w_down):
    depth = w_in.shape[0]
    bp, L, _ = x_prompt.shape
    n = x_sample.shape[0]
    n_pool = cache_k.shape[1]
    n_pages = page_table.shape[1]

    z_end = SSM_D_INNER
    xbc_end = z_end + SSM_CONV_DIM
    dt_end = xbc_end + SSM_HEADS
    q_end = dt_end + ATT_Q_COLS
    k_end = q_end + ATT_K_COLS
    v_end = k_end + ATT_V_COLS
    w_dt = jnp.pad(w_in[:, :, xbc_end:dt_end], ((0, 0), (0, 0), (0, DT_PAD - SSM_HEADS)))
    w_q = w_in[:, :, dt_end:q_end] * (ATT_QK_DIM ** -0.5 * LOG2E)
    w_cat = jnp.concatenate(
        [w_in[:, :, :xbc_end], w_q, w_in[:, :, q_end:], w_dt], axis=-1).astype(bf16)
    w_ada_b = w_ada.astype(bf16)
    wb_b = w_branch.astype(bf16)
    wo_b = w_out.astype(bf16)
    wgu_b = w_gu.astype(bf16)
    wd_b = w_down.astype(bf16)
    row = lambda a: a.reshape(depth, 1, a.shape[-1])
    pad_heads = lambda a: jnp.pad(a, ((0, 0), (0, DT_PAD - SSM_HEADS))).reshape(depth, 1, DT_PAD)
    dtb_p = pad_heads(dt_bias)
    alog_p = pad_heads(a_log)
    dsk_f = jnp.repeat(d_skip, SSM_HEADDIM, axis=-1).reshape(depth, 1, SSM_D_INNER)
    g_pre_mix, g_post_mix, g_pre_ffn, g_post_ffn = map(row, (g_pre_mix, g_post_mix, g_pre_ffn,
                                                            g_post_ffn))
    cb_r, ng_r, sg_r = row(conv_b), row(ssm_norm_g), row(subln_g)
    slopes_np, nslope_np = _alibi_tables()
    slopes = jnp.asarray(slopes_np)
    nslope = jnp.asarray(nslope_np)
    e_np = _expand_matrix()
    e_mat = jnp.asarray(e_np, dtype=bf16)

    sg_col = subln_g.reshape(depth, ATT_V_DIM, 1)
    cache_kt = jnp.transpose(cache_k, (0, 1, 3, 4, 5, 2)).reshape(depth, n_pool, ATT_K_COLS, PAGE_SIZE)
    cache_vr = cache_v.reshape(depth, n_pool, PAGE_SIZE * ATT_KV_HEADS, ATT_V_DIM)
    state4 = state_ssm.reshape(depth, n, SSM_D_INNER, SSM_D_STATE)

    mod_all = _ada_mod(jnp.concatenate([c_prompt, c_sample], axis=0), w_ada_b, b_ada)

    tl_p = _pick(L, 256)
    tl_wide = _pick(L, 512)
    t_ssd = _pick(L, 128)
    tq = _pick(L, 512)
    pps = _pick(n_pages, PAGES_PER_STEP)

    xp = x_prompt
    xs = x_sample.reshape(1, n, D_MODEL)
    outs = {name: [] for name in ("conv_p", "ssm_p", "conv_s", "ssm_s")}
    kv_p = kv_s = None
    for l in range(depth):
        lam_init = 0.8 - 0.6 * math.exp(-0.3 * l)
        mod_p = mod_all[l, :, :bp].reshape(6, bp, 1, D_MODEL)
        mod_s = mod_all[l, :, bp:].reshape(6, 1, n, D_MODEL)

        z, xbc, q, k_all, v_all, kb, vb, gates, dt = _in_proj(xp, mod_p, g_pre_mix, w_cat, l, depth,
                                                              tl_p, kv_p)
        kv_p = (k_all, v_all)
        y_ssm, st = _ssd_prompt(xbc, z, dt, conv_w, cb_r, dtb_p, alog_p, dsk_f, ng_r, e_mat,
                                l, t_ssd)
        y_att = _attn_prompt(q, kb, vb, slopes, lam_vecs, sg_col, l, lam_init, tq,
                             ATT_HEADS_PER_STEP)
        x1 = _merge(y_ssm, y_att, gates, xp, mod_p, g_post_mix, wb_b, wo_b, l, tl_wide)
        xp = _ffn(x1, mod_p, g_pre_ffn, g_post_ffn, wgu_b, wd_b, l, tl_wide)
        outs["conv_p"].append(xbc[:, L - (SSM_CONV - 1):, :])
        outs["ssm_p"].append(st.reshape(bp, SSM_HEADS, SSM_HEADDIM, SSM_D_STATE))

        z, xbc, q, k_all, v_all, kb, vb, gates, dt = _in_proj(xs, mod_s, g_pre_mix, w_cat, l, depth,
                                                              n, kv_s)
        kv_s = (k_all, v_all)
        k_new = k_all[l, 0].T
        v_new = v_all[l, 0].reshape(n, ATT_V_COLS)
        sc_t = jnp.transpose(state_conv[l], (1, 0, 2))
        xs_c, bc, dec3, xdt2, conv_new = _sample_pre(xbc[0], sc_t, dt[0], conv_w, cb_r, dtb_p,
                                                     alog_p, e_mat, l)
        dec_t = jnp.transpose(dec3, (2, 0, 1)).reshape(SSM_D_INNER, 3 * n)
        xdt_t = jnp.transpose(xdt2, (2, 0, 1)).reshape(SSM_D_INNER, 2 * n)
        st_s, y_raw = _sample_state(state4, dec_t, xdt_t, bc, l)
        y_ssm = _sample_gate(y_raw.reshape(n, SSM_D_INNER), xs_c, z[0], dsk_f, ng_r, l)
        qmat = _query_matrix(q[0])
        first_row = lambda a: jnp.pad(a.astype(bf16)[:, None, :],
                                      ((0, 0), (0, SUBLANES - 1), (0, 0)))
        y_att = _paged_attn(page_table, qmat, first_row(k_new), first_row(v_new), nslope, lam_vecs,
                            sg_r, cache_kt, cache_vr, l, lam_init, pps)
        x1 = _merge(y_ssm.reshape(1, n, SSM_D_INNER), y_att.reshape(1, n, ATT_WIDTH), gates, xs,
                    mod_s, g_post_mix, wb_b, wo_b, l, n)
        xs = _ffn(x1, mod_s, g_pre_ffn, g_post_ffn, wgu_b, wd_b, l, n)
        outs["conv_s"].append(jnp.transpose(conv_new, (1, 0, 2)))
        outs["ssm_s"].append(st_s.reshape(n, SSM_HEADS, SSM_HEADDIM, SSM_D_STATE))

    def keys_out(k_all, rows, length):
        k6 = k_all.reshape(depth, rows, ATT_KV_HEADS, 2, ATT_QK_DIM, length)
        return jnp.transpose(k6, (0, 1, 5, 2, 3, 4))

    def values_out(v_all, rows, length):
        return v_all.reshape(depth, rows, length, ATT_KV_HEADS, ATT_V_DIM)

    stack = {name: jnp.stack(o) for name, o in outs.items()}
    return (xp, xs.reshape(n, 1, D_MODEL),
            keys_out(kv_p[0], bp, L), values_out(kv_p[1], bp, L), stack["conv_p"], stack["ssm_p"],
            keys_out(kv_s[0], 1, n).reshape(depth, n, 1, ATT_KV_HEADS, 2, ATT_QK_DIM),
            values_out(kv_s[1], 1, n).reshape(depth, n, 1, ATT_KV_HEADS, ATT_V_DIM),
            stack["conv_s"], stack["ssm_s"])
```

```python
import functools
import math

import numpy as np
import jax
import jax.numpy as jnp
from jax import lax
from jax.experimental import pallas as pl
from jax.experimental.pallas import tpu as pltpu

D_MODEL = 1024
SSM_HEADDIM = 64
SSM_HEADS = 16
SSM_GROUPS = 2
SSM_D_STATE = 128
SSM_CONV = 4
SSM_D_INNER = SSM_HEADS * SSM_HEADDIM
SSM_BC = SSM_GROUPS * SSM_D_STATE
SSM_CONV_DIM = SSM_D_INNER + 2 * SSM_BC
SSM_GROUP_WIDTH = SSM_D_INNER // SSM_GROUPS
ATT_HEADS = 8
ATT_KV_HEADS = 4
ATT_GROUP = ATT_HEADS // ATT_KV_HEADS
ATT_QK_DIM = 64
ATT_V_DIM = 128
ATT_WIDTH = ATT_HEADS * ATT_V_DIM
ATT_Q_COLS = ATT_HEADS * 2 * ATT_QK_DIM
ATT_K_COLS = ATT_KV_HEADS * 2 * ATT_QK_DIM
ATT_V_COLS = ATT_KV_HEADS * ATT_V_DIM
ATT_SCORE_COLS = ATT_KV_HEADS * ATT_GROUP * 2
FFN_HIDDEN = 2816
RMS_EPS = 1e-6
PAGE_SIZE = 128
LANES = 128
SUBLANES = 8
DT_PAD = LANES

_SEG_Z = (0, SSM_D_INNER)
_SEG_XBC = (_SEG_Z[1], _SEG_Z[1] + SSM_CONV_DIM)
_SEG_Q = (_SEG_XBC[1], _SEG_XBC[1] + ATT_Q_COLS)
_SEG_K = (_SEG_Q[1], _SEG_Q[1] + ATT_K_COLS)
_SEG_V = (_SEG_K[1], _SEG_K[1] + ATT_V_COLS)
_SEG_G = (_SEG_V[1], _SEG_V[1] + 2 * D_MODEL)
_SEG_DT = (_SEG_G[1], _SEG_G[1] + DT_PAD)
ATT_KW_COLS = ATT_KV_HEADS * 2 * LANES
IN_COLS_PAD = _SEG_DT[1]
ATT_HEADS_PER_STEP = 4
PAGES_PER_STEP = 32
ONES_ROWS = 2 * SUBLANES
POS_TERMS = 3
LOG2E = math.log2(math.e)

VMEM_LIMIT = 56 * 1024 * 1024

f32 = jnp.float32
bf16 = jnp.bfloat16


def _cparams(sem):
    return pltpu.CompilerParams(dimension_semantics=sem, vmem_limit_bytes=VMEM_LIMIT)


def _rms(x):
    return x * lax.rsqrt(jnp.mean(x * x, axis=-1, keepdims=True) + RMS_EPS)


def _silu(x):
    return x * jax.nn.sigmoid(x)


def _softplus(x):
    return jnp.maximum(x, 0.0) + jnp.log(1.0 + jnp.exp(-jnp.abs(x)))


def _split_bf16(x, terms):
    parts = []
    rem = x
    for _ in range(terms):
        p = rem.astype(bf16)
        parts.append(p)
        rem = rem - p.astype(f32)
    return parts


def _dot(a, b):
    return jnp.dot(a, b, preferred_element_type=f32)


def _dot_nt(a, b):
    return lax.dot_general(a, b, (((1,), (1,)), ((), ())), preferred_element_type=f32)


def _dot_tn(a, b):
    return lax.dot_general(a, b, (((0,), (0,)), ((), ())), preferred_element_type=f32)


def _dot_split(x, w, terms):
    acc = None
    for p in _split_bf16(x, terms):
        t = _dot(p, w)
        acc = t if acc is None else acc + t
    return acc


def _ada_kernel(c_ref, w_ref, b_ref, o_ref):
    s = _silu(c_ref[...])
    o_ref[...] = _dot(s.astype(bf16), w_ref[...]) + b_ref[...]


def _ada_mod(c_all, w_ada_b, b_ada):
    depth = w_ada_b.shape[0]
    rows = c_all.shape[0]
    return pl.pallas_call(
        _ada_kernel,
        grid=(depth, 6),
        in_specs=[
            pl.BlockSpec((rows, D_MODEL), lambda l, j: (0, 0)),
            pl.BlockSpec((None, D_MODEL, D_MODEL), lambda l, j: (l, 0, j)),
            pl.BlockSpec((None, 1, D_MODEL), lambda l, j: (l, 0, j)),
        ],
        out_specs=pl.BlockSpec((None, None, rows, D_MODEL), lambda l, j: (l, j, 0, 0)),
        out_shape=jax.ShapeDtypeStruct((depth, 6, rows, D_MODEL), f32),
        compiler_params=_cparams(("arbitrary", "arbitrary")),
        name="ada_mod",
    )(c_all, w_ada_b, b_ada.reshape(depth, 1, 6 * D_MODEL))


def _inproj_kernel(x_ref, mod_ref, g_ref, w_ref, *rest):
    z_ref, xbc_ref, q_ref, k_ref, v_ref, kb_ref, vb_ref, gt_ref, dt_ref = rest[-9:]
    h = _rms(x_ref[...]) * g_ref[...]
    h = h * (1.0 + mod_ref[1]) + mod_ref[0]
    hb = h.astype(bf16)

    def seg(lo_hi):
        return _dot(hb, w_ref[:, lo_hi[0]:lo_hi[1]])

    z_ref[...] = seg(_SEG_Z)
    xbc_ref[...] = seg(_SEG_XBC)
    q_ref[...] = seg(_SEG_Q).astype(bf16)
    kk = seg(_SEG_K)
    k_ref[...] = kk.T
    rows = x_ref.shape[0]
    lane = lax.broadcasted_iota(jnp.int32, (rows, LANES), 1)
    pos = pl.program_id(1) * rows + lax.broadcasted_iota(jnp.int32, (rows, LANES), 0)
    part = jnp.where((lane & 1) == 0, ((pos >> 8) << 8).astype(f32), (pos & 255).astype(f32))
    off = lane & (ATT_QK_DIM - 1)
    extra = jnp.where(off < 2 * POS_TERMS, part, 0.0)
    for g in range(ATT_KV_HEADS):
        kv = kk[:, g * LANES:(g + 1) * LANES]
        kb_ref[:, 2 * g * LANES:(2 * g + 1) * LANES] = jnp.where(
            lane < ATT_QK_DIM, kv, extra).astype(bf16)
        kb_ref[:, (2 * g + 1) * LANES:(2 * g + 2) * LANES] = jnp.where(
            lane >= ATT_QK_DIM, kv, extra).astype(bf16)
    vv = seg(_SEG_V)
    for g in range(ATT_KV_HEADS):
        v_ref[pl.ds(g, rows, stride=ATT_KV_HEADS), :] = vv[:, g * ATT_V_DIM:(g + 1) * ATT_V_DIM]
    vb_ref[...] = vv.T.astype(bf16)
    gt_ref[...] = seg(_SEG_G)
    dt_ref[...] = seg(_SEG_DT)


K_OUT, V_OUT, VT_OUT = 3, 4, 6


def _in_proj(x, mod, g_pre, w_cat, layer, depth, tl, kv_buffers=None):
    nb, L, _ = x.shape
    r = mod.shape[2]
    widths = [(SSM_D_INNER, f32), (SSM_CONV_DIM, f32), (ATT_Q_COLS, bf16), (ATT_K_COLS, f32),
              (ATT_V_COLS, f32), (ATT_KW_COLS, bf16), (ATT_V_COLS, bf16), (2 * D_MODEL, f32),
              (DT_PAD, f32)]
    tok = lambda w: pl.BlockSpec((None, tl, w), lambda b, i: (b, i, 0))
    out_specs = [tok(w) for w, _ in widths]
    out_shape = [jax.ShapeDtypeStruct((nb, L, w), dt) for w, dt in widths]
    out_specs[VT_OUT] = pl.BlockSpec((None, ATT_V_COLS, tl), lambda b, i: (b, 0, i))
    out_shape[VT_OUT] = jax.ShapeDtypeStruct((nb, ATT_V_COLS, L), bf16)
    out_specs[K_OUT] = pl.BlockSpec((None, None, ATT_K_COLS, tl), lambda b, i: (layer, b, 0, i))
    out_shape[K_OUT] = jax.ShapeDtypeStruct((depth, nb, ATT_K_COLS, L), f32)
    out_specs[V_OUT] = pl.BlockSpec((None, None, tl * ATT_KV_HEADS, ATT_V_DIM),
                                    lambda b, i: (layer, b, i, 0))
    out_shape[V_OUT] = jax.ShapeDtypeStruct((depth, nb, L * ATT_KV_HEADS, ATT_V_DIM), f32)
    in_specs = [
        tok(D_MODEL),
        pl.BlockSpec((6, None, r, D_MODEL), lambda b, i: (0, b, 0, 0)),
        pl.BlockSpec((None, 1, D_MODEL), lambda b, i: (layer, 0, 0)),
        pl.BlockSpec((None, D_MODEL, IN_COLS_PAD), lambda b, i: (layer, 0, 0),
                     pipeline_mode=pl.Buffered(1)),
    ]
    args = [x, mod, g_pre, w_cat]
    aliases = {}
    if kv_buffers is not None:
        aliases = {len(args): K_OUT, len(args) + 1: V_OUT}
        in_specs += [pl.BlockSpec(memory_space=pl.ANY), pl.BlockSpec(memory_space=pl.ANY)]
        args += list(kv_buffers)
    return pl.pallas_call(
        _inproj_kernel,
        grid=(nb, L // tl),
        in_specs=in_specs,
        out_specs=out_specs,
        out_shape=out_shape,
        input_output_aliases=aliases,
        compiler_params=_cparams(("parallel", "parallel")),
        name="in_proj",
    )(*args)


def _ssd_kernel(xbc_ref, z_ref, dt_ref, cw_ref, cb_ref, dtb_ref, alog_ref, dsk_ref, ng_ref, e_ref,
                y_ref, st_ref, xpad_ref, *, T):
    c = pl.program_id(1)

    @pl.when(c == 0)
    def _():
        xpad_ref[0:SUBLANES, :] = jnp.zeros((SUBLANES, SSM_CONV_DIM), f32)
        st_ref[...] = jnp.zeros_like(st_ref)

    xr = xbc_ref[...]
    xpad_ref[SUBLANES:SUBLANES + T, :] = xr
    acc = xr * cw_ref[SSM_CONV - 1:SSM_CONV, :]
    for lag in range(1, SSM_CONV):
        acc = acc + (xpad_ref[SUBLANES - lag:SUBLANES - lag + T, :]
                     * cw_ref[SSM_CONV - 1 - lag:SSM_CONV - lag, :])
    xpad_ref[0:SUBLANES, :] = xpad_ref[T:T + SUBLANES, :]
    conv = _silu(acc + cb_ref[...])
    xs = conv[:, :SSM_D_INNER]
    bm = conv[:, SSM_D_INNER:SSM_D_INNER + SSM_BC].astype(bf16)
    cm = conv[:, SSM_D_INNER + SSM_BC:].astype(bf16)

    dt = _softplus(dt_ref[...] + dtb_ref[...])
    da = dt * (-jnp.exp(alog_ref[...]))
    row = lax.broadcasted_iota(jnp.int32, (T, T), 0)
    col = lax.broadcasted_iota(jnp.int32, (T, T), 1)
    causal = row >= col
    tri = causal.astype(bf16)
    acum = None
    for p in _split_bf16(da, 3):
        t = _dot(tri, p)
        acum = t if acum is None else acum + t
    acum_t = acum.T
    last = acum[T - 1:T, :]
    ea = jnp.exp(acum)
    to_end = jnp.exp(last - acum)

    e = e_ref[...]
    dt_f = _dot_split(dt, e, 2)
    ea_f = _dot_split(ea, e, 2)
    te_f = _dot_split(to_end, e, 2)
    xdt = xs * dt_f
    xdt_b = xdt.astype(bf16)
    xdtw_b = (xdt * te_f).astype(bf16)

    cd = jnp.exp(jnp.broadcast_to(acum_t[:, T - 1:T], (LANES, SSM_D_STATE)))

    hpg = SSM_HEADS // SSM_GROUPS
    pair_lane = lax.broadcasted_iota(jnp.int32, (T, 2 * SSM_HEADDIM), 1)
    y_parts = []
    for g in range(SSM_GROUPS):
        bg = bm[:, g * SSM_D_STATE:(g + 1) * SSM_D_STATE]
        cg = cm[:, g * SSM_D_STATE:(g + 1) * SSM_D_STATE]
        cb = _dot_nt(cg, bg)
        rows = slice(g * SSM_GROUP_WIDTH, (g + 1) * SSM_GROUP_WIDTH)
        s_in = st_ref[rows, :]
        y_off = _dot_nt(cg, s_in.astype(bf16)) * ea_f[:, rows]

        def decay_mix(h):
            seg = acum[:, h:h + 1] - acum_t[h:h + 1, :]
            dec = jnp.exp(jnp.where(causal, seg, -jnp.inf))
            return (cb * dec).astype(bf16)

        diag = []
        for hh in range(0, hpg, 2):
            h = g * hpg + hh
            xpair = xdt_b[:, h * SSM_HEADDIM:(h + 2) * SSM_HEADDIM]
            zero = jnp.zeros_like(xpair)
            w2 = jnp.concatenate([jnp.where(pair_lane < SSM_HEADDIM, xpair, zero),
                                  jnp.where(pair_lane >= SSM_HEADDIM, xpair, zero)], axis=0)
            m2 = jnp.concatenate([decay_mix(h), decay_mix(h + 1)], axis=1)
            diag.append(_dot(m2, w2))
        y_parts.append(jnp.concatenate(diag, axis=1) + y_off)
        st_new = _dot_tn(xdtw_b[:, rows], bg)
        for hh in range(hpg):
            h = g * hpg + hh
            hrows = slice(h * SSM_HEADDIM, (h + 1) * SSM_HEADDIM)
            lrows = slice(hh * SSM_HEADDIM, (hh + 1) * SSM_HEADDIM)
            st_ref[hrows, :] = s_in[lrows, :] * cd[h:h + 1, :] + st_new[lrows, :]
    y = jnp.concatenate(y_parts, axis=1) + dsk_ref[...] * xs
    yz = y * _silu(z_ref[...])
    outs = []
    for g in range(SSM_GROUPS):
        outs.append(_rms(yz[:, g * SSM_GROUP_WIDTH:(g + 1) * SSM_GROUP_WIDTH]))
    y_ref[...] = (jnp.concatenate(outs, axis=1) * ng_ref[...]).astype(bf16)


def _ssd_prompt(xbc, z, dt, cw, cb, dtb, alog, dsk, ng, e, layer, T):
    nb, L, _ = xbc.shape
    tok = lambda w: pl.BlockSpec((None, T, w), lambda b, i: (b, i, 0))
    par = lambda r, w: pl.BlockSpec((None, r, w), lambda b, i: (layer, 0, 0))
    return pl.pallas_call(
        functools.partial(_ssd_kernel, T=T),
        grid=(nb, L // T),
        in_specs=[tok(SSM_CONV_DIM), tok(SSM_D_INNER), tok(DT_PAD),
                  par(SSM_CONV, SSM_CONV_DIM), par(1, SSM_CONV_DIM), par(1, DT_PAD), par(1, DT_PAD),
                  par(1, SSM_D_INNER), par(1, SSM_D_INNER),
                  pl.BlockSpec((LANES, SSM_D_INNER), lambda b, i: (0, 0))],
        out_specs=[tok(SSM_D_INNER),
                   pl.BlockSpec((None, SSM_D_INNER, SSM_D_STATE), lambda b, i: (b, 0, 0))],
        out_shape=[jax.ShapeDtypeStruct((nb, L, SSM_D_INNER), bf16),
                   jax.ShapeDtypeStruct((nb, SSM_D_INNER, SSM_D_STATE), f32)],
        scratch_shapes=[pltpu.VMEM((T + SUBLANES, SSM_CONV_DIM), f32)],
        compiler_params=_cparams(("parallel", "arbitrary")),
        name="ssd_prompt",
    )(xbc, z, dt, cw, cb, dtb, alog, dsk, ng, e)


def _lambda_value(lam_ref, lam_init):
    lv = lam_ref[...]
    a = jnp.sum(lv[0:1, :] * lv[1:2, :], axis=-1, keepdims=True)
    b = jnp.sum(lv[2:3, :] * lv[3:4, :], axis=-1, keepdims=True)
    return jnp.exp(a) - jnp.exp(b) + lam_init


def _sum_row_groups(x):
    acc = x[0:SUBLANES, :]
    for i in range(1, x.shape[0] // SUBLANES):
        acc = acc + x[i * SUBLANES:(i + 1) * SUBLANES, :]
    return acc


def _attn_kernel(qi_ref, ki_ref, q_ref, k_ref, v_ref, slope_ref, lam_ref, sgc_ref, o_ref,
                 m_ref, acc_ref, qa_ref, s_ref, p_ref, *, tq, tk, lam_init, gu):
    g0 = pl.program_id(1) * gu
    step = pl.program_id(2)
    qi = qi_ref[step]
    ki = ki_ref[step]
    half = ATT_QK_DIM
    qw = ATT_GROUP * 2 * half
    kw = 2 * LANES
    dv = ATT_V_DIM

    @pl.when(ki == 0)
    def _():
        m_ref[...] = jnp.full_like(m_ref, -jnp.inf)
        acc_ref[...] = jnp.zeros_like(acc_ref)
        lane = lax.broadcasted_iota(jnp.int32, (tq, 2 * half), 1)
        for gl in range(gu):
            for r in range(ATT_GROUP):
                qv = q_ref[:, gl * qw + r * 2 * half:gl * qw + (r + 1) * 2 * half]
                a0 = jnp.zeros((tq, 2 * half), f32)
                a1 = jnp.zeros((tq, 2 * half), f32)
                for t in range(POS_TERMS):
                    term = slope_ref[g0 + gl, r * POS_TERMS + t]
                    a0 = jnp.where((lane >> 1) == (half >> 1) + t, term, a0)
                    a1 = jnp.where((lane >> 1) == t, term, a1)
                qa_ref[4 * gl + 2 * r] = jnp.where(lane < half, qv, a0.astype(bf16))
                qa_ref[4 * gl + 2 * r + 1] = jnp.where(lane >= half, qv, a1.astype(bf16))

    def run(masked, gl):
        ones = jnp.ones((ONES_ROWS, tk), bf16)
        v1 = jnp.concatenate([v_ref[gl * dv:(gl + 1) * dv, :], ones], axis=0)
        ka = (k_ref[:, gl * kw:gl * kw + 2 * half], k_ref[:, gl * kw + 2 * half:(gl + 1) * kw])
        hk = tk // 2
        skip = masked and tq == tk and (tq // 2) % LANES == 0
        if masked:
            keep = (lax.broadcasted_iota(jnp.int32, (tk, tq), 0)
                    <= lax.broadcasted_iota(jnp.int32, (tk, tq), 1))
        grp = 2 * SUBLANES
        alphas = []
        if not skip:
            for c in range(4):
                idx = 4 * gl + c
                s = _dot_nt(ka[c % 2], qa_ref[idx])
                if masked:
                    s = jnp.where(keep, s, -jnp.inf)
                s_ref[idx] = s
                m_old = m_ref[idx]
                m_new = jnp.maximum(m_old, jnp.max(s, axis=0, keepdims=True))
                alphas.append(jnp.exp2(m_old - m_new))
                m_ref[idx] = m_new
            for c in range(4):
                idx = 4 * gl + c
                mg = jnp.broadcast_to(m_ref[idx], (grp, tq))
                for i in range(tk // grp):
                    rows = slice(i * grp, (i + 1) * grp)
                    p_ref[idx, rows, :] = jnp.exp2(s_ref[idx, rows, :] - mg).astype(bf16)
            for c in range(4):
                idx = 4 * gl + c
                acc_ref[idx] = alphas[c] * acc_ref[idx] + _dot(v1, p_ref[idx])
            return
        for c in range(4):
            idx = 4 * gl + c
            kc = ka[c % 2]
            s_top = jnp.where(keep[:hk, :], _dot_nt(kc[:hk, :], qa_ref[idx]), -jnp.inf)
            s_bot = jnp.where(keep[hk:, hk:], _dot_nt(kc[hk:, :], qa_ref[idx, hk:, :]), -jnp.inf)
            s_ref[idx, :hk, :] = s_top
            s_ref[idx, hk:, hk:] = s_bot
            m_top = jnp.max(s_top, axis=0, keepdims=True)
            m_cur = jnp.concatenate(
                [m_top[:, :hk], jnp.maximum(m_top[:, hk:], jnp.max(s_bot, axis=0, keepdims=True))],
                axis=1)
            m_old = m_ref[idx]
            m_new = jnp.maximum(m_old, m_cur)
            alphas.append(jnp.exp2(m_old - m_new))
            m_ref[idx] = m_new
        for c in range(4):
            idx = 4 * gl + c
            mg = jnp.broadcast_to(m_ref[idx], (grp, tq))
            for i in range(hk // grp):
                rows = slice(i * grp, (i + 1) * grp)
                p_ref[idx, rows, :] = jnp.exp2(s_ref[idx, rows, :] - mg).astype(bf16)
            for i in range(hk // grp, tk // grp):
                rows = slice(i * grp, (i + 1) * grp)
                p_ref[idx, rows, hk:] = jnp.exp2(s_ref[idx, rows, hk:] - mg[:, hk:]).astype(bf16)
        for c in range(4):
            idx = 4 * gl + c
            acc_ref[idx] = alphas[c] * acc_ref[idx] + _dot(v1[:, :hk], p_ref[idx, :hk, :])
            acc_ref[idx, :, hk:] = acc_ref[idx, :, hk:] + _dot(v1[:, hk:], p_ref[idx, hk:, hk:])

    def finish(gl):
        lam = _lambda_value(lam_ref, lam_init)
        for r in range(ATT_GROUP):
            i0 = 4 * gl + 2 * r
            o = (acc_ref[i0, :dv, :] / acc_ref[i0, dv:dv + 1, :]
                 - lam * (acc_ref[i0 + 1, :dv, :] / acc_ref[i0 + 1, dv:dv + 1, :]))
            o = o * lax.rsqrt(jnp.mean(o * o, axis=0, keepdims=True) + RMS_EPS)
            o = o * sgc_ref[...] * (1.0 - lam_init)
            col = (gl * ATT_GROUP + r) * dv
            o_ref[:, col:col + dv] = o.T.astype(bf16)

    @pl.when(ki < qi)
    def _():
        for gl in range(gu):
            run(False, gl)

    @pl.when(ki == qi)
    def _():
        for gl in range(gu):
            run(True, gl)
            finish(gl)


def _attn_prompt(q, kb, vt, slopes, lam_vecs, subln_col, layer, lam_init, tq, gu):
    nb, L, _ = q.shape
    assert L <= 65536, "key positions are split into two 8-bit parts"
    nq = L // tq
    pairs = [(i, j) for i in range(nq) for j in range(i + 1)]
    qi_tab = jnp.asarray(np.array([a for a, _ in pairs], np.int32))
    ki_tab = jnp.asarray(np.array([b for _, b in pairs], np.int32))
    qw = gu * ATT_GROUP * 2 * ATT_QK_DIM
    kw = gu * 2 * LANES
    ow = gu * ATT_GROUP * ATT_V_DIM
    grid_spec = pltpu.PrefetchScalarGridSpec(
        num_scalar_prefetch=2,
        grid=(nb, ATT_KV_HEADS // gu, len(pairs)),
        in_specs=[
            pl.BlockSpec((None, tq, qw), lambda b, g, p, qi, ki: (b, qi[p], g)),
            pl.BlockSpec((None, tq, kw), lambda b, g, p, qi, ki: (b, ki[p], g)),
            pl.BlockSpec((None, gu * ATT_V_DIM, tq), lambda b, g, p, qi, ki: (b, g, ki[p])),
            pl.BlockSpec(memory_space=pltpu.SMEM),
            pl.BlockSpec((None, 4, ATT_QK_DIM), lambda b, g, p, qi, ki: (layer, 0, 0)),
            pl.BlockSpec((None, ATT_V_DIM, 1), lambda b, g, p, qi, ki: (layer, 0, 0)),
        ],
        out_specs=pl.BlockSpec((None, tq, ow), lambda b, g, p, qi, ki: (b, qi[p], g)),
        scratch_shapes=[pltpu.VMEM((4 * gu, 1, tq), f32),
                        pltpu.VMEM((4 * gu, ATT_V_DIM + ONES_ROWS, tq), f32),
                        pltpu.VMEM((4 * gu, tq, 2 * ATT_QK_DIM), bf16),
                        pltpu.VMEM((4 * gu, tq, tq), f32), pltpu.VMEM((4 * gu, tq, tq), bf16)],
    )
    return pl.pallas_call(
        functools.partial(_attn_kernel, tq=tq, tk=tq, lam_init=lam_init, gu=gu),
        grid_spec=grid_spec,
        out_shape=jax.ShapeDtypeStruct((nb, L, ATT_WIDTH), bf16),
        compiler_params=_cparams(("parallel", "parallel", "arbitrary")),
        name="attn_prompt",
    )(qi_tab, ki_tab, q, kb, vt, slopes, lam_vecs, subln_col)


def _merge_kernel(ys_ref, ya_ref, gt_ref, x_ref, mod_ref, g_ref, wb_ref, wo_ref, o_ref):
    bp0 = _dot(ys_ref[...], wb_ref[0])
    bp1 = _dot(ya_ref[...], wb_ref[1])
    gt = gt_ref[...]
    merged = (jax.nn.sigmoid(gt[:, :D_MODEL]) * bp0 + jax.nn.sigmoid(gt[:, D_MODEL:]) * bp1)
    m = _dot(merged.astype(bf16), wo_ref[...])
    o_ref[...] = x_ref[...] + mod_ref[2] * (_rms(m) * g_ref[...])


def _merge(y_ssm, y_att, gates, x, mod, g_post, wb, wo, layer, tl):
    nb, L, _ = x.shape
    r = mod.shape[2]
    tok = lambda w: pl.BlockSpec((None, tl, w), lambda b, i: (b, i, 0))
    return pl.pallas_call(
        _merge_kernel,
        grid=(nb, L // tl),
        in_specs=[
            tok(SSM_D_INNER), tok(ATT_WIDTH), tok(2 * D_MODEL), tok(D_MODEL),
            pl.BlockSpec((6, None, r, D_MODEL), lambda b, i: (0, b, 0, 0)),
            pl.BlockSpec((None, 1, D_MODEL), lambda b, i: (layer, 0, 0)),
            pl.BlockSpec((None, 2, ATT_WIDTH, D_MODEL), lambda b, i: (layer, 0, 0, 0),
                         pipeline_mode=pl.Buffered(1)),
            pl.BlockSpec((None, D_MODEL, D_MODEL), lambda b, i: (layer, 0, 0),
                         pipeline_mode=pl.Buffered(1)),
        ],
        out_specs=tok(D_MODEL),
        out_shape=jax.ShapeDtypeStruct((nb, L, D_MODEL), f32),
        compiler_params=_cparams(("parallel", "parallel")),
        name="merge_out",
    )(y_ssm, y_att, gates, x, mod, g_post, wb, wo)


def _ffn_kernel(x_ref, mod_ref, g1_ref, g2_ref, wgu_ref, wd_ref, o_ref):
    x = x_ref[...]
    h = _rms(x) * g1_ref[...]
    hb = (h * (1.0 + mod_ref[4]) + mod_ref[3]).astype(bf16)
    gg = _dot(hb, wgu_ref[:, :FFN_HIDDEN])
    uu = _dot(hb, wgu_ref[:, FFN_HIDDEN:])
    f = _dot((_silu(gg) * uu).astype(bf16), wd_ref[...])
    o_ref[...] = x + mod_ref[5] * (_rms(f) * g2_ref[...])


def _ffn(x, mod, g_pre, g_post, wgu, wd, layer, tl):
    nb, L, _ = x.shape
    r = mod.shape[2]
    tok = pl.BlockSpec((None, tl, D_MODEL), lambda b, i: (b, i, 0))
    par = pl.BlockSpec((None, 1, D_MODEL), lambda b, i: (layer, 0, 0))
    return pl.pallas_call(
        _ffn_kernel,
        grid=(nb, L // tl),
        in_specs=[
            tok,
            pl.BlockSpec((6, None, r, D_MODEL), lambda b, i: (0, b, 0, 0)),
            par, par,
            pl.BlockSpec((None, D_MODEL, 2 * FFN_HIDDEN), lambda b, i: (layer, 0, 0),
                         pipeline_mode=pl.Buffered(1)),
            pl.BlockSpec((None, FFN_HIDDEN, D_MODEL), lambda b, i: (layer, 0, 0),
                         pipeline_mode=pl.Buffered(1)),
        ],
        out_specs=tok,
        out_shape=jax.ShapeDtypeStruct((nb, L, D_MODEL), f32),
        compiler_params=_cparams(("parallel", "parallel")),
        name="ffn",
    )(x, mod, g_pre, g_post, wgu, wd)


def _sample_pre_kernel(xbc_ref, sc_ref, dt_ref, cw_ref, cb_ref, dtb_ref, alog_ref, e_ref,
                       xs_ref, bc_ref, dec_ref, xdt_ref, cn_ref):
    xr = xbc_ref[...]
    acc = xr * cw_ref[SSM_CONV - 1:SSM_CONV, :]
    for j in range(SSM_CONV - 1):
        acc = acc + sc_ref[j] * cw_ref[j:j + 1, :]
    conv = _silu(acc + cb_ref[...])
    xs = conv[:, :SSM_D_INNER]
    xs_ref[...] = xs
    bc_ref[...] = conv[:, SSM_D_INNER:]
    for j in range(SSM_CONV - 2):
        cn_ref[j] = sc_ref[j + 1]
    cn_ref[SSM_CONV - 2] = xr
    dt = _softplus(dt_ref[...] + dtb_ref[...])
    da = dt * (-jnp.exp(alog_ref[...]))
    e = e_ref[...]
    dec = jnp.exp(_dot_split(da, e, 3))
    xdt = xs * _dot_split(dt, e, 3)
    for j, p in enumerate(_split_bf16(dec, 3)):
        dec_ref[j] = p
    for j, p in enumerate(_split_bf16(xdt, 2)):
        xdt_ref[j] = p


def _sample_pre(xbc, sc_t, dt, cw, cb, dtb, alog, e, layer):
    n = xbc.shape[0]
    full = lambda *s: pl.BlockSpec(s, lambda i: (0,) * len(s))
    par = lambda r, w: pl.BlockSpec((None, r, w), lambda i: (layer, 0, 0))
    return pl.pallas_call(
        _sample_pre_kernel,
        grid=(1,),
        in_specs=[full(n, SSM_CONV_DIM), full(SSM_CONV - 1, n, SSM_CONV_DIM), full(n, DT_PAD),
                  par(SSM_CONV, SSM_CONV_DIM), par(1, SSM_CONV_DIM), par(1, DT_PAD), par(1, DT_PAD),
                  full(LANES, SSM_D_INNER)],
        out_specs=[full(n, SSM_D_INNER), full(n, 2 * SSM_BC), full(3, n, SSM_D_INNER),
                   full(2, n, SSM_D_INNER), full(SSM_CONV - 1, n, SSM_CONV_DIM)],
        out_shape=[jax.ShapeDtypeStruct((n, SSM_D_INNER), f32),
                   jax.ShapeDtypeStruct((n, 2 * SSM_BC), f32),
                   jax.ShapeDtypeStruct((3, n, SSM_D_INNER), bf16),
                   jax.ShapeDtypeStruct((2, n, SSM_D_INNER), bf16),
                   jax.ShapeDtypeStruct((SSM_CONV - 1, n, SSM_CONV_DIM), f32)],
        compiler_params=_cparams(("arbitrary",)),
        name="sample_pre",
    )(xbc, sc_t, dt, cw, cb, dtb, alog, e)


def _sample_state_kernel(s_ref, dec_ref, xdt_ref, bc_ref, so_ref, y_ref, *, n):
    b = pl.program_id(0)

    def onehot(copies):
        rr = lax.broadcasted_iota(jnp.int32, (copies * n, SSM_D_STATE), 0)
        hit = rr == b
        for c in range(1, copies):
            hit = jnp.logical_or(hit, rr == b + c * n)
        return hit.astype(bf16)

    dm = _dot(dec_ref[...], onehot(3))
    xm = _dot(xdt_ref[...], onehot(2))
    brow = bc_ref[pl.ds(b, 1), :]
    y_parts = []
    for g in range(SSM_GROUPS):
        rows = slice(g * SSM_GROUP_WIDTH, (g + 1) * SSM_GROUP_WIDTH)
        bg = brow[:, g * SSM_D_STATE:(g + 1) * SSM_D_STATE]
        cg = brow[:, SSM_BC + g * SSM_D_STATE:SSM_BC + (g + 1) * SSM_D_STATE]
        s_new = s_ref[rows, :] * dm[rows, :] + xm[rows, :] * bg
        so_ref[rows, :] = s_new
        c8 = jnp.broadcast_to(cg, (SUBLANES, SSM_D_STATE)).astype(bf16)
        y_parts.append(_dot_nt(c8, s_new.astype(bf16))[0:1, :])
    y_ref[...] = jnp.concatenate(y_parts, axis=1)


def _sample_state(state, dec_t, xdt_t, bc, layer):
    n = bc.shape[0]
    return pl.pallas_call(
        functools.partial(_sample_state_kernel, n=n),
        grid=(n,),
        in_specs=[
            pl.BlockSpec((None, None, SSM_D_INNER, SSM_D_STATE), lambda b: (layer, b, 0, 0)),
            pl.BlockSpec((SSM_D_INNER, 3 * n), lambda b: (0, 0)),
            pl.BlockSpec((SSM_D_INNER, 2 * n), lambda b: (0, 0)),
            pl.BlockSpec((n, 2 * SSM_BC), lambda b: (0, 0)),
        ],
        out_specs=[pl.BlockSpec((None, SSM_D_INNER, SSM_D_STATE), lambda b: (b, 0, 0)),
                   pl.BlockSpec((None, 1, SSM_D_INNER), lambda b: (b, 0, 0))],
        out_shape=[jax.ShapeDtypeStruct((n, SSM_D_INNER, SSM_D_STATE), f32),
                   jax.ShapeDtypeStruct((n, 1, SSM_D_INNER), f32)],
        compiler_params=_cparams(("parallel",)),
        name="sample_state",
    )(state, dec_t, xdt_t, bc)


def _sample_gate_kernel(y_ref, xs_ref, z_ref, dsk_ref, ng_ref, o_ref):
    y = y_ref[...] + dsk_ref[...] * xs_ref[...]
    yz = y * _silu(z_ref[...])
    outs = []
    for g in range(SSM_GROUPS):
        outs.append(_rms(yz[:, g * SSM_GROUP_WIDTH:(g + 1) * SSM_GROUP_WIDTH]))
    o_ref[...] = (jnp.concatenate(outs, axis=1) * ng_ref[...]).astype(bf16)


def _sample_gate(y, xs, z, dsk, ng, layer):
    n = y.shape[0]
    full = pl.BlockSpec((n, SSM_D_INNER), lambda i: (0, 0))
    par = pl.BlockSpec((None, 1, SSM_D_INNER), lambda i: (layer, 0, 0))
    return pl.pallas_call(
        _sample_gate_kernel,
        grid=(1,),
        in_specs=[full, full, full, par, par],
        out_specs=full,
        out_shape=jax.ShapeDtypeStruct((n, SSM_D_INNER), bf16),
        compiler_params=_cparams(("arbitrary",)),
        name="sample_gate",
    )(y, xs, z, dsk, ng)


def _paged_attn_kernel(pt_ref, qm_ref, kn_ref, vn_ref, ns_ref, lam_ref, sg_ref, *rest, pps,
                       n_steps, past, lam_init):
    del pt_ref
    k_refs = rest[:pps]
    v_refs = rest[pps:2 * pps]
    o_ref = rest[2 * pps]
    m_ref, l_ref, acc_ref, s_scr = rest[2 * pps + 1:]
    j = pl.program_id(1)
    qm = qm_ref[...]
    nslope = ns_ref[...]

    @pl.when(j == 0)
    def _():
        s0 = _dot(kn_ref[...], qm)
        first = lax.broadcasted_iota(jnp.int32, s0.shape, 0) == 0
        s0 = jnp.where(first, s0, -jnp.inf)
        m0 = jnp.max(s0, axis=0, keepdims=True)
        p0 = jnp.exp2(s0 - m0)
        m_ref[...] = m0
        l_ref[...] = jnp.sum(p0, axis=0, keepdims=True)
        acc_ref[...] = _dot_tn(vn_ref[...], p0.astype(bf16))

    base = j * (pps * PAGE_SIZE)
    key = lax.broadcasted_iota(jnp.int32, (PAGE_SIZE, LANES), 0)
    for i in range(pps):
        s = _dot_tn(k_refs[i][...].astype(bf16), qm)
        dist = (past - (base + i * PAGE_SIZE) - key).astype(f32)
        s_scr[i * PAGE_SIZE:(i + 1) * PAGE_SIZE, :] = s + dist * nslope
    s_all = s_scr[...]
    m_old = m_ref[...]
    m_new = jnp.maximum(m_old, jnp.max(s_all, axis=0, keepdims=True))
    alpha = jnp.exp2(m_old - m_new)
    pe = jnp.exp2(s_all - m_new)
    l_ref[...] = alpha * l_ref[...] + jnp.sum(pe, axis=0, keepdims=True)
    pb = pe.astype(bf16)
    for g in range(ATT_KV_HEADS):
        pv = None
        for i in range(pps):
            vg = v_refs[i][pl.ds(g, PAGE_SIZE, stride=ATT_KV_HEADS), :].astype(bf16)
            t = _dot_tn(vg, pb[i * PAGE_SIZE:(i + 1) * PAGE_SIZE, :])
            pv = t if pv is None else pv + t
        rows = slice(g * ATT_V_DIM, (g + 1) * ATT_V_DIM)
        acc_ref[rows, :] = acc_ref[rows, :] * alpha + pv
    m_ref[...] = m_new

    @pl.when(j == n_steps - 1)
    def _():
        lam = _lambda_value(lam_ref, lam_init)
        inv_l = 1.0 / l_ref[...]
        for g in range(ATT_KV_HEADS):
            ot = (acc_ref[g * ATT_V_DIM:(g + 1) * ATT_V_DIM, :] * inv_l).T
            for r in range(ATT_GROUP):
                c0 = g * (ATT_GROUP * 2) + r * 2
                o = ot[c0:c0 + 1, :] - lam * ot[c0 + 1:c0 + 2, :]
                o = _rms(o) * sg_ref[...] * (1.0 - lam_init)
                h = g * ATT_GROUP + r
                o_ref[:, h * ATT_V_DIM:(h + 1) * ATT_V_DIM] = o.astype(bf16)


def _paged_attn(page_table, qmat, k_new, v_new, nslope, lam_vecs, subln_g, cache_kt, cache_v,
                layer, lam_init, pps):
    n, n_pages = page_table.shape
    n_steps = n_pages // pps
    past = n_pages * PAGE_SIZE

    def k_spec(i):
        return pl.BlockSpec((None, None, ATT_K_COLS, PAGE_SIZE),
                            lambda b, j, pt: (layer, pt[b, j * pps + i], 0, 0))

    def v_spec(i):
        return pl.BlockSpec((None, None, PAGE_SIZE * ATT_KV_HEADS, ATT_V_DIM),
                            lambda b, j, pt: (layer, pt[b, j * pps + i], 0, 0))

    grid_spec = pltpu.PrefetchScalarGridSpec(
        num_scalar_prefetch=1,
        grid=(n, n_steps),
        in_specs=[
            pl.BlockSpec((None, ATT_K_COLS, LANES), lambda b, j, pt: (b, 0, 0)),
            pl.BlockSpec((None, SUBLANES, ATT_K_COLS), lambda b, j, pt: (b, 0, 0)),
            pl.BlockSpec((None, SUBLANES, ATT_V_COLS), lambda b, j, pt: (b, 0, 0)),
            pl.BlockSpec((1, LANES), lambda b, j, pt: (0, 0)),
            pl.BlockSpec((None, 4, ATT_QK_DIM), lambda b, j, pt: (layer, 0, 0)),
            pl.BlockSpec((None, 1, ATT_V_DIM), lambda b, j, pt: (layer, 0, 0)),
        ] + [k_spec(i) for i in range(pps)] + [v_spec(i) for i in range(pps)],
        out_specs=pl.BlockSpec((None, 1, ATT_WIDTH), lambda b, j, pt: (b, 0, 0)),
        scratch_shapes=[pltpu.VMEM((1, LANES), f32), pltpu.VMEM((1, LANES), f32),
                        pltpu.VMEM((ATT_V_COLS, LANES), f32),
                        pltpu.VMEM((pps * PAGE_SIZE, LANES), f32)],
    )
    return pl.pallas_call(
        functools.partial(_paged_attn_kernel, pps=pps, n_steps=n_steps, past=past,
                          lam_init=lam_init),
        grid_spec=grid_spec,
        out_shape=jax.ShapeDtypeStruct((n, 1, ATT_WIDTH), bf16),
        compiler_params=_cparams(("parallel", "arbitrary")),
        name="paged_attn",
    )(page_table, qmat, k_new, v_new, nslope, lam_vecs, subln_g,
      *([cache_kt] * pps), *([cache_v] * pps))


def _alibi_tables():
    import ml_dtypes
    h = np.arange(1, ATT_HEADS + 1, dtype=np.float32)
    slopes = (2.0 ** (-8.0 * h / ATT_HEADS)).astype(np.float32) * np.float32(LOG2E)
    terms = np.zeros((ATT_HEADS, POS_TERMS), np.float32)
    rem = slopes.copy()
    for t in range(POS_TERMS):
        terms[:, t] = rem.astype(ml_dtypes.bfloat16).astype(np.float32)
        rem = rem - terms[:, t]
    per_col = np.zeros((1, LANES), np.float32)
    per_col[0, :ATT_SCORE_COLS] = -np.repeat(slopes, 2)
    return terms.reshape(ATT_KV_HEADS, ATT_GROUP * POS_TERMS), per_col


def _expand_matrix():
    e = np.zeros((LANES, SSM_D_INNER), np.float32)
    for h in range(SSM_HEADS):
        e[h, h * SSM_HEADDIM:(h + 1) * SSM_HEADDIM] = 1.0
    return e


def _pick(n, prefer):
    t = min(prefer, n)
    while n % t:
        t //= 2
    return t


def _query_matrix(q):
    n = q.shape[0]
    q5 = q.reshape(n, ATT_KV_HEADS, ATT_GROUP, 2, ATT_QK_DIM)
    eg = jnp.eye(ATT_KV_HEADS, dtype=q.dtype)
    ec = jnp.eye(2, dtype=q.dtype)
    qm = jnp.einsum('bgrcd,gh,ce->bgcdhre', q5, eg, ec)
    qm = qm.reshape(n, ATT_K_COLS, ATT_SCORE_COLS)
    return jnp.pad(qm, ((0, 0), (0, 0), (0, LANES - ATT_SCORE_COLS)))


def kernel(x_prompt, x_sample, cache_k, cache_v, state_conv, state_ssm, page_table, c_prompt,
           c_sample, w_ada, b_ada, g_pre_mix, g_post_mix, g_pre_ffn, g_post_ffn, w_in, conv_w,
           conv_b, dt_bias, a_log, d_skip, ssm_norm_g, lam_vecs, subln_g, w_branch, w_out, w_gu,
           w_down):
    depth = w_in.shape[0]
    bp, L, _ = x_prompt.shape
    n = x_sample.shape[0]
    n_pool = cache_k.shape[1]
    n_pages = page_table.shape[1]

    z_end = SSM_D_INNER
    xbc_end = z_end + SSM_CONV_DIM
    dt_end = xbc_end + SSM_HEADS
    q_end = dt_end + ATT_Q_COLS
    k_end = q_end + ATT_K_COLS
    v_end = k_end + ATT_V_COLS
    w_dt = jnp.pad(w_in[:, :, xbc_end:dt_end], ((0, 0), (0, 0), (0, DT_PAD - SSM_HEADS)))
    w_q = w_in[:, :, dt_end:q_end] * (ATT_QK_DIM ** -0.5 * LOG2E)
    w_cat = jnp.concatenate(
        [w_in[:, :, :xbc_end], w_q, w_in[:, :, q_end:], w_dt], axis=-1).astype(bf16)
    w_ada_b = w_ada.astype(bf16)
    wb_b = w_branch.astype(bf16)
    wo_b = w_out.astype(bf16)
    wgu_b = w_gu.astype(bf16)
    wd_b = w_down.astype(bf16)
    row = lambda a: a.reshape(depth, 1, a.shape[-1])
    pad_heads = lambda a: jnp.pad(a, ((0, 0), (0, DT_PAD - SSM_HEADS))).reshape(depth, 1, DT_PAD)
    dtb_p = pad_heads(dt_bias)
    alog_p = pad_heads(a_log)
    dsk_f = jnp.repeat(d_skip, SSM_HEADDIM, axis=-1).reshape(depth, 1, SSM_D_INNER)
    g_pre_mix, g_post_mix, g_pre_ffn, g_post_ffn = map(row, (g_pre_mix, g_post_mix, g_pre_ffn,
                                                            g_post_ffn))
    cb_r, ng_r, sg_r = row(conv_b), row(ssm_norm_g), row(subln_g)
    slopes_np, nslope_np = _alibi_tables()
    slopes = jnp.asarray(slopes_np)
    nslope = jnp.asarray(nslope_np)
    e_np = _expand_matrix()
    e_mat = jnp.asarray(e_np, dtype=bf16)

    sg_col = subln_g.reshape(depth, ATT_V_DIM, 1)
    cache_kt = jnp.transpose(cache_k, (0, 1, 3, 4, 5, 2)).reshape(depth, n_pool, ATT_K_COLS, PAGE_SIZE)
    cache_vr = cache_v.reshape(depth, n_pool, PAGE_SIZE * ATT_KV_HEADS, ATT_V_DIM)
    state4 = state_ssm.reshape(depth, n, SSM_D_INNER, SSM_D_STATE)

    mod_all = _ada_mod(jnp.concatenate([c_prompt, c_sample], axis=0), w_ada_b, b_ada)

    tl_p = _pick(L, 256)
    tl_wide = _pick(L, 512)
    t_ssd = _pick(L, 128)
    tq = _pick(L, 512)
    pps = _pick(n_pages, PAGES_PER_STEP)

    xp = x_prompt
    xs = x_sample.reshape(1, n, D_MODEL)
    outs = {name: [] for name in ("conv_p", "ssm_p", "conv_s", "ssm_s")}
    kv_p = kv_s = None
    for l in range(depth):
        lam_init = 0.8 - 0.6 * math.exp(-0.3 * l)
        mod_p = mod_all[l, :, :bp].reshape(6, bp, 1, D_MODEL)
        mod_s = mod_all[l, :, bp:].reshape(6, 1, n, D_MODEL)

        z, xbc, q, k_all, v_all, kb, vb, gates, dt = _in_proj(xp, mod_p, g_pre_mix, w_cat, l, depth,
                                                              tl_p, kv_p)
        kv_p = (k_all, v_all)
        y_ssm, st = _ssd_prompt(xbc, z, dt, conv_w, cb_r, dtb_p, alog_p, dsk_f, ng_r, e_mat,
                                l, t_ssd)
        y_att = _attn_prompt(q, kb, vb, slopes, lam_vecs, sg_col, l, lam_init, tq,
                             ATT_HEADS_PER_STEP)
        x1 = _merge(y_ssm, y_att, gates, xp, mod_p, g_post_mix, wb_b, wo_b, l, tl_wide)
        xp = _ffn(x1, mod_p, g_pre_ffn, g_post_ffn, wgu_b, wd_b, l, tl_wide)
        outs["conv_p"].append(xbc[:, L - (SSM_CONV - 1):, :])
        outs["ssm_p"].append(st.reshape(bp, SSM_HEADS, SSM_HEADDIM, SSM_D_STATE))

        z, xbc, q, k_all, v_all, kb, vb, gates, dt = _in_proj(xs, mod_s, g_pre_mix, w_cat, l, depth,
                                                              n, kv_s)
        kv_s = (k_all, v_all)
        k_new = k_all[l, 0].T
        v_new = v_all[l, 0].reshape(n, ATT_V_COLS)
        sc_t = jnp.transpose(state_conv[l], (1, 0, 2))
        xs_c, bc, dec3, xdt2, conv_new = _sample_pre(xbc[0], sc_t, dt[0], conv_w, cb_r, dtb_p,
                                                     alog_p, e_mat, l)
        dec_t = jnp.transpose(dec3, (2, 0, 1)).reshape(SSM_D_INNER, 3 * n)
        xdt_t = jnp.transpose(xdt2, (2, 0, 1)).reshape(SSM_D_INNER, 2 * n)
        st_s, y_raw = _sample_state(state4, dec_t, xdt_t, bc, l)
        y_ssm = _sample_gate(y_raw.reshape(n, SSM_D_INNER), xs_c, z[0], dsk_f, ng_r, l)
        qmat = _query_matrix(q[0])
        first_row = lambda a: jnp.pad(a.astype(bf16)[:, None, :],
                                      ((0, 0), (0, SUBLANES - 1), (0, 0)))
        y_att = _paged_attn(page_table, qmat, first_row(k_new), first_row(v_new), nslope, lam_vecs,
                            sg_r, cache_kt, cache_vr, l, lam_init, pps)
        x1 = _merge(y_ssm.reshape(1, n, SSM_D_INNER), y_att.reshape(1, n, ATT_WIDTH), gates, xs,
                    mod_s, g_post_mix, wb_b, wo_b, l, n)
        xs = _ffn(x1, mod_s, g_pre_ffn, g_post_ffn, wgu_b, wd_b, l, n)
        outs["conv_s"].append(jnp.transpose(conv_new, (1, 0, 2)))
        outs["ssm_s"].append(st_s.reshape(n, SSM_HEADS, SSM_HEADDIM, SSM_D_STATE))

    def keys_out(k_all, rows, length):
        k6 = k_all.reshape(depth, rows, ATT_KV_HEADS, 2, ATT_QK_DIM, length)
        return jnp.transpose(k6, (0, 1, 5, 2, 3, 4))

    def values_out(v_all, rows, length):
        return v_all.reshape(depth, rows, length, ATT_KV_HEADS, ATT_V_DIM)

    stack = {name: jnp.stack(o) for name, o in outs.items()}
    return (xp, xs.reshape(n, 1, D_MODEL),
            keys_out(kv_p[0], bp, L), values_out(kv_p[1], bp, L), stack["conv_p"], stack["ssm_p"],
            keys_out(kv_s[0], 1, n).reshape(depth, n, 1, ATT_KV_HEADS, 2, ATT_QK_DIM),
            values_out(kv_s[1], 1, n).reshape(depth, n, 1, ATT_KV_HEADS, ATT_V_DIM),
            stack["conv_s"], stack["ssm_s"])
```
